```python
import math, functools
import jax, jax.numpy as jnp
from jax import lax
import numpy as np

D_MODEL = 1024
BATCH = 1
SEQ = 16384
DEPTH = 1
DEC_BATCH = 32
DEC_SEQ = 4
PAST_LEN = 16384
PAGE_SIZE = 128

A_DQK = 64
A_DV = 2 * A_DQK
A_HEADS = D_MODEL // A_DV
A_W = A_HEADS * A_DV
A_SCALE = A_DQK ** -0.5
Q_BLOCK = 128
B_HEADS = 4
B_DK = D_MODEL // (2 * B_HEADS)
B_DV = D_MODEL // B_HEADS
B_WK = B_HEADS * B_DK
B_WV = B_HEADS * B_DV
B_SCALE = B_DK ** -0.5
GATE_RANK = 16
GATE_TAU = 16.0
GLA_CHUNK = 64
D_FF = -(-8 * D_MODEL // (3 * 256)) * 256
EPS = 1e-6
IN_SIZES = (A_W, A_W, A_W, B_WK, B_WK, B_WV, GATE_RANK, B_WV, D_MODEL, D_MODEL)
IN_W = sum(IN_SIZES)

kernel_name = 'hybrid_diffattn_gla_decode_step'


def rms_norm(x, g):
    xf = x.astype(jnp.float32)
    y = xf * lax.rsqrt(jnp.mean(xf * xf, axis=-1, keepdims=True) + EPS)
    return (y * g.astype(jnp.float32)).astype(x.dtype)


def split_columns(z):
    parts, off = [], 0
    for s in IN_SIZES:
        parts.append(z[..., off:off + s])
        off += s
    return parts


def diff_attn_prompt(q, k, v, lam):
    bsz, seq = q.shape[:2]
    n_blk = seq // Q_BLOCK
    kf = k.astype(jnp.float32)
    vf = v.astype(jnp.float32)
    q_blocks = (q.astype(jnp.float32) * A_SCALE).reshape(bsz, n_blk, Q_BLOCK, A_HEADS, 2, A_DQK).swapaxes(0, 1)
    k_pos = jnp.arange(seq)

    def one_block(args):
        qi, blk = args
        s = jnp.einsum('bqhmd,bkhmd->bhmqk', qi, kf)
        q_pos = blk * Q_BLOCK + jnp.arange(Q_BLOCK)
        s = jnp.where(k_pos[None, :] <= q_pos[:, None], s, -jnp.inf)
        p = jax.nn.softmax(s, axis=-1)
        p_diff = p[:, :, 0] - lam * p[:, :, 1]
        return jnp.einsum('bhqk,bkhv->bqhv', p_diff, vf)

    o = lax.map(one_block, (q_blocks, jnp.arange(n_blk)))
    return o.swapaxes(0, 1).reshape(bsz, seq, A_HEADS, A_DV)


def diff_attn_sample(q, k, v, cache_k, cache_v, page_table, lam):
    bsz, t_new = q.shape[:2]
    qf = q.astype(jnp.float32) * A_SCALE
    s = jnp.einsum('bthmd,bshmd->bhmts', qf, k.astype(jnp.float32))
    causal = jnp.tril(jnp.ones((t_new, t_new), dtype=bool))
    s = jnp.where(causal, s, -jnp.inf)
    m = jnp.max(s, axis=-1)
    p = jnp.exp(s - m[..., None])
    l = jnp.sum(p, axis=-1)
    acc = jnp.einsum('bhmts,bshv->bhmtv', p, v.astype(jnp.float32))

    def page_step(carry, pt):
        m, l, acc = carry
        kp = cache_k[pt].astype(jnp.float32).reshape(bsz, PAGE_SIZE, A_HEADS, 2, A_DQK)
        vp = cache_v[pt].astype(jnp.float32)
        sp = jnp.einsum('bthmd,bshmd->bhmts', qf, kp)
        m_new = jnp.maximum(m, jnp.max(sp, axis=-1))
        corr = jnp.exp(m - m_new)
        pp = jnp.exp(sp - m_new[..., None])
        l = l * corr + jnp.sum(pp, axis=-1)
        acc = acc * corr[..., None] + jnp.einsum('bhmts,bshv->bhmtv', pp, vp)
        return (m_new, l, acc), None

    (m, l, acc), _ = lax.scan(page_step, (m, l, acc), page_table.T)
    o = acc / l[..., None]
    o = o[:, :, 0] - lam * o[:, :, 1]
    return o.transpose(0, 2, 1, 3)


def gla_scan(q, k, v, log_a, s0):
    bsz, seq = q.shape[:2]
    c = GLA_CHUNK if seq % GLA_CHUNK == 0 else seq
    n = seq // c

    def to_chunks(t):
        return t.astype(jnp.float32).reshape(bsz, n, c, *t.shape[2:]).swapaxes(0, 1)

    causal = jnp.tril(jnp.ones((c, c), dtype=bool))[None, :, :, None, None]

    def step(s, inp):
        qc, kc, vc, ac = inp
        b = jnp.cumsum(ac, axis=1)
        decay = jnp.exp(jnp.where(causal, b[:, :, None] - b[:, None, :], -jnp.inf))
        attn = jnp.sum(qc[:, :, None] * kc[:, None] * decay, axis=-1)
        o = jnp.einsum('btsh,bshv->bthv', attn, vc) + jnp.einsum('bthd,bhdv->bthv', qc * jnp.exp(b), s)
        b_last = b[:, -1:]
        s = jnp.exp(b_last[:, 0])[..., None] * s + jnp.einsum('bshd,bshv->bhdv', kc * jnp.exp(b_last - b), vc)
        return s, o

    s, o = lax.scan(step, s0.astype(jnp.float32), (to_chunks(q), to_chunks(k), to_chunks(v), to_chunks(log_a)))
    return o.swapaxes(0, 1).reshape(bsz, seq, B_HEADS, B_DV), s.astype(s0.dtype)


def hybrid_layer(x, attend, gla_s0, lam_init, g_mix_pre, w_in, w_gk, b_gk, g_attn_sub, g_gla_sub,
                 w_out, g_mix_post, g_ffn_pre, w_gate, w_up, w_down, g_ffn_post):
    bsz, seq, _ = x.shape
    h = rms_norm(x, g_mix_pre)
    qa, ka, va, qb, kb, vb, g_lr, r, gate_a, gate_b = split_columns(h @ w_in)
    qa = qa.reshape(bsz, seq, A_HEADS, 2, A_DQK)
    ka = ka.reshape(bsz, seq, A_HEADS, 2, A_DQK)
    va = va.reshape(bsz, seq, A_HEADS, A_DV)
    o_a = attend(qa, ka, va)
    o_a = rms_norm(o_a, g_attn_sub) * (1.0 - lam_init)
    log_a = jax.nn.log_sigmoid((g_lr @ w_gk + b_gk).astype(jnp.float32)) / GATE_TAU
    o_b, s_final = gla_scan((qb * B_SCALE).reshape(bsz, seq, B_HEADS, B_DK),
                            kb.reshape(bsz, seq, B_HEADS, B_DK),
                            vb.reshape(bsz, seq, B_HEADS, B_DV),
                            log_a.reshape(bsz, seq, B_HEADS, B_DK), gla_s0)
    o_b = rms_norm(o_b, g_gla_sub) * jax.nn.silu(r.astype(jnp.float32)).reshape(bsz, seq, B_HEADS, B_DV)
    merged = (jax.nn.sigmoid(gate_a.astype(jnp.float32)) * o_a.reshape(bsz, seq, A_W)
              + jax.nn.sigmoid(gate_b.astype(jnp.float32)) * o_b.reshape(bsz, seq, B_WV))
    x = x + rms_norm(merged.astype(x.dtype) @ w_out, g_mix_post)
    h2 = rms_norm(x, g_ffn_pre)
    f = (jax.nn.silu(h2 @ w_gate) * (h2 @ w_up)) @ w_down
    x = x + rms_norm(f, g_ffn_post)
    return x, ka.reshape(bsz, seq, A_HEADS, 2 * A_DQK), va, s_final


def setup_inputs(seed: int = 0) -> dict:
    key = jax.random.key(seed)
    ks = jax.random.split(key, 24)
    n_pages = PAST_LEN // PAGE_SIZE
    n_used = DEC_BATCH * n_pages
    n_pool = (5 * n_used + 3) // 4
    nrm = jax.random.normal
    f32 = jnp.float32

    def gain(k, n):
        return 1.0 + 0.02 * nrm(k, (DEPTH, n), f32)

    page_table = jax.random.permutation(ks[5], n_pool)[:n_used].reshape(DEC_BATCH, n_pages).astype(jnp.int32)
    return {
        'x_prompt': nrm(ks[0], (BATCH, SEQ, D_MODEL), f32),
        'x_sample': nrm(ks[1], (DEC_BATCH, DEC_SEQ, D_MODEL), f32),
        'cache_k': nrm(ks[2], (DEPTH, n_pool, PAGE_SIZE, A_HEADS, 2 * A_DQK), f32),
        'cache_v': nrm(ks[3], (DEPTH, n_pool, PAGE_SIZE, A_HEADS, A_DV), f32),
        'state_gla': nrm(ks[4], (DEPTH, DEC_BATCH, B_HEADS, B_DK, B_DV), f32),
        'page_table': page_table,
        'g_mix_pre': gain(ks[6], D_MODEL),
        'w_in': nrm(ks[7], (DEPTH, D_MODEL, IN_W), f32) * D_MODEL ** -0.5,
        'w_gk': nrm(ks[8], (DEPTH, GATE_RANK, B_WK), f32) * GATE_RANK ** -0.5,
        'b_gk': 0.1 * nrm(ks[9], (DEPTH, B_WK), f32),
        'lambda_q1': 0.1 * nrm(ks[10], (DEPTH, A_DQK), f32),
        'lambda_k1': 0.1 * nrm(ks[11], (DEPTH, A_DQK), f32),
        'lambda_q2': 0.1 * nrm(ks[12], (DEPTH, A_DQK), f32),
        'lambda_k2': 0.1 * nrm(ks[13], (DEPTH, A_DQK), f32),
        'g_attn_sub': gain(ks[14], A_DV),
        'g_gla_sub': gain(ks[15], B_DV),
        'w_out': nrm(ks[16], (DEPTH, D_MODEL, D_MODEL), f32) * D_MODEL ** -0.5,
        'g_mix_post': gain(ks[17], D_MODEL),
        'g_ffn_pre': gain(ks[18], D_MODEL),
        'w_gate': nrm(ks[19], (DEPTH, D_MODEL, D_FF), f32) * D_MODEL ** -0.5,
        'w_up': nrm(ks[20], (DEPTH, D_MODEL, D_FF), f32) * D_MODEL ** -0.5,
        'w_down': nrm(ks[21], (DEPTH, D_FF, D_MODEL), f32) * D_FF ** -0.5,
        'g_ffn_post': gain(ks[22], D_MODEL),
    }


def reference(x_prompt, x_sample, cache_k, cache_v, state_gla, page_table, g_mix_pre, w_in, w_gk, b_gk,
              lambda_q1, lambda_k1, lambda_q2, lambda_k2, g_attn_sub, g_gla_sub, w_out, g_mix_post,
              g_ffn_pre, w_gate, w_up, w_down, g_ffn_post):
    x_p, x_s = x_prompt, x_sample
    kp_l, vp_l, sp_l, ks_l, vs_l, ss_l = [], [], [], [], [], []
    for l in range(DEPTH):
        lam_init = 0.8 - 0.6 * math.exp(-0.3 * l)
        lam = (jnp.exp(jnp.sum(lambda_q1[l].astype(jnp.float32) * lambda_k1[l].astype(jnp.float32)))
               - jnp.exp(jnp.sum(lambda_q2[l].astype(jnp.float32) * lambda_k2[l].astype(jnp.float32)))
               + lam_init)
        weights = (g_mix_pre[l], w_in[l], w_gk[l], b_gk[l], g_attn_sub[l], g_gla_sub[l], w_out[l],
                   g_mix_post[l], g_ffn_pre[l], w_gate[l], w_up[l], w_down[l], g_ffn_post[l])
        s0_p = jnp.zeros((x_p.shape[0], B_HEADS, B_DK, B_DV), jnp.float32)
        x_p, kp, vp, sp = hybrid_layer(x_p, functools.partial(diff_attn_prompt, lam=lam), s0_p, lam_init, *weights)
        attend_s = functools.partial(diff_attn_sample, cache_k=cache_k[l], cache_v=cache_v[l],
                                     page_table=page_table, lam=lam)
        x_s, ks_, vs_, ss_ = hybrid_layer(x_s, attend_s, state_gla[l], lam_init, *weights)
        kp_l.append(kp); vp_l.append(vp); sp_l.append(sp)
        ks_l.append(ks_); vs_l.append(vs_); ss_l.append(ss_)
    k_prompt = jnp.stack(kp_l); v_prompt = jnp.stack(vp_l); gla_prompt = jnp.stack(sp_l)
    k_sample = jnp.stack(ks_l); v_sample = jnp.stack(vs_l); gla_sample = jnp.stack(ss_l)
    return (x_p, x_s, k_prompt, v_prompt, gla_prompt, k_sample, v_sample, gla_sample)
```

```python
import functools
import math

import jax
import jax.numpy as jnp
from jax import lax
from jax.experimental import pallas as pl
from jax.experimental.pallas import tpu as pltpu

A_DQK = 64
A_DV = 2 * A_DQK
B_HEADS = 4
GATE_TAU = 16.0
EPS = 1e-6

LANES = 128
SUBLANES = 8
VMEM_LIMIT_BYTES = 56 * 1024 * 1024

NEG_BIG = -1e30
F32 = jnp.float32
BF16 = jnp.bfloat16


def _params(*semantics):
    return pltpu.CompilerParams(dimension_semantics=semantics, vmem_limit_bytes=VMEM_LIMIT_BYTES)


def _rms(x, g):
    return x * lax.rsqrt(jnp.mean(x * x, axis=-1, keepdims=True) + EPS) * g


def _lambda(lq1, lk1, lq2, lk2, lam_init):
    s1 = jnp.sum(lq1 * lk1, axis=-1, keepdims=True)
    s2 = jnp.sum(lq2 * lk2, axis=-1, keepdims=True)
    return jnp.exp(s1) - jnp.exp(s2) + lam_init


def _in_proj_kernel(x_ref, g_ref, wq_t_ref, wv_t_ref, wg_ref, wgk_ref, bgk_ref, wm_ref,
                    z_ref, qt_ref, vt_ref, kb_ref, la_ref, h_ref, *, q_scale):
    j = pl.program_id(1)

    @pl.when(j == 0)
    def _():
        h = _rms(x_ref[...], g_ref[...]).astype(BF16)
        h_ref[...] = h
        nt = (((1,), (1,)), ((), ()))
        qt = lax.dot_general(wq_t_ref[...], h, nt, preferred_element_type=F32)
        qt_ref[...] = (qt * q_scale).astype(BF16)
        vt = lax.dot_general(wv_t_ref[...], h, nt, preferred_element_type=F32)
        vt_ref[...] = vt.astype(BF16)
        g_lr = jnp.dot(h, wg_ref[...], preferred_element_type=F32)
        gk = jnp.dot(g_lr.astype(BF16), wgk_ref[...], preferred_element_type=F32) + bgk_ref[...]
        la_ref[...] = jax.nn.log_sigmoid(gk) / GATE_TAU

    @pl.when(j > 0)
    def _():
        z = jnp.dot(h_ref[...], wm_ref[...], preferred_element_type=F32)
        z_ref[...] = z

        @pl.when(j == 1)
        def _():
            kb_ref[...] = z.astype(BF16)


def _in_proj(x, g, wq_t, wv_t, wg, wgk, bgk, wm, *, tm):
    t, d = x.shape
    n_tiles = wm.shape[1] // d
    wk = wgk.shape[1]
    grid = (t // tm, n_tiles + 1)
    const = lambda i, j: (0, 0)
    return pl.pallas_call(
        functools.partial(_in_proj_kernel, q_scale=A_DQK ** -0.5),
        grid=grid,
        in_specs=[
            pl.BlockSpec((tm, d), lambda i, j: (i, 0)),
            pl.BlockSpec((1, d), const),
            pl.BlockSpec((d, d), const),
            pl.BlockSpec((d, d), const),
            pl.BlockSpec((d, LANES), const),
            pl.BlockSpec((LANES, wk), const),
            pl.BlockSpec((1, wk), const),
            pl.BlockSpec((d, d), lambda i, j: (0, jnp.maximum(j - 1, 0))),
        ],
        out_specs=[
            pl.BlockSpec((tm, d), lambda i, j: (i, jnp.maximum(j - 1, 0))),
            pl.BlockSpec((d, tm), lambda i, j: (0, i)),
            pl.BlockSpec((d, tm), lambda i, j: (0, i)),
            pl.BlockSpec((tm, d), lambda i, j: (i, 0)),
            pl.BlockSpec((tm, wk), lambda i, j: (i, 0)),
        ],
        out_shape=[
            jax.ShapeDtypeStruct((t, n_tiles * d), F32),
            jax.ShapeDtypeStruct((d, t), BF16),
            jax.ShapeDtypeStruct((d, t), BF16),
            jax.ShapeDtypeStruct((t, d), BF16),
            jax.ShapeDtypeStruct((t, wk), F32),
        ],
        scratch_shapes=[pltpu.VMEM((tm, d), BF16)],
        compiler_params=_params("parallel", "arbitrary"),
        name="in_proj",
    )(x, g, wq_t, wv_t, wg, wgk, bgk, wm)


def _attn_prompt_kernel(qt_ref, k_ref, vt_ref, lq1_ref, lk1_ref, lq2_ref, lk2_ref, gcol_ref,
                        o_ref, m_ref, l_ref, acc_ref, *, blk, lam_init):
    i = pl.program_id(1)
    q = qt_ref[...]
    row = lax.broadcasted_iota(jnp.int32, q.shape, 0)
    zero = jnp.zeros_like(q)
    qs = (jnp.where(row < A_DQK, q, zero), jnp.where(row >= A_DQK, q, zero))

    m_ref[...] = jnp.full(m_ref.shape, NEG_BIG, F32)
    l_ref[...] = jnp.zeros(l_ref.shape, F32)
    acc_ref[...] = jnp.zeros(acc_ref.shape, F32)

    def step(kb, masked):
        start = pl.multiple_of(kb * blk, blk)
        k = k_ref[pl.ds(start, blk), :]
        vt = vt_ref[:, pl.ds(start, blk)]
        for mi in range(2):
            s = jnp.dot(k, qs[mi], preferred_element_type=F32)
            if masked:
                kpos = lax.broadcasted_iota(jnp.int32, s.shape, 0)
                qpos = lax.broadcasted_iota(jnp.int32, s.shape, 1)
                s = jnp.where(kpos <= qpos, s, NEG_BIG)
            m_old = m_ref[mi]
            m_new = jnp.maximum(m_old, jnp.max(s, axis=0, keepdims=True))
            alpha = jnp.exp(m_old - m_new)
            p = jnp.exp(s - m_new)
            l_ref[mi] = alpha * l_ref[mi] + jnp.sum(p, axis=0, keepdims=True)
            acc_ref[mi] = alpha * acc_ref[mi] + jnp.dot(vt, p.astype(BF16), preferred_element_type=F32)
            m_ref[mi] = m_new

    def body(kb, carry):
        step(kb, False)
        return carry

    lax.fori_loop(0, i, body, 0)
    step(i, True)

    lam = _lambda(lq1_ref[...], lk1_ref[...], lq2_ref[...], lk2_ref[...], lam_init)
    o = acc_ref[0] / l_ref[0] - lam * (acc_ref[1] / l_ref[1])
    ms = jnp.mean(o * o, axis=0, keepdims=True)
    y = o * lax.rsqrt(ms + EPS) * gcol_ref[...] * (1.0 - lam_init)
    o_ref[...] = y.T


def _attn_prompt(qt, kb, vt, lq1, lk1, lq2, lk2, gcol, *, blk, lam_init):
    d, t = qt.shape
    heads = d // A_DV
    vec = pl.BlockSpec((1, A_DQK), lambda h, i: (0, 0))
    return pl.pallas_call(
        functools.partial(_attn_prompt_kernel, blk=blk, lam_init=lam_init),
        grid=(heads, t // blk),
        in_specs=[
            pl.BlockSpec((A_DV, blk), lambda h, i: (h, i)),
            pl.BlockSpec((t, A_DV), lambda h, i: (0, h)),
            pl.BlockSpec((A_DV, t), lambda h, i: (h, 0)),
            vec, vec, vec, vec,
            pl.BlockSpec((A_DV, 1), lambda h, i: (0, 0)),
        ],
        out_specs=pl.BlockSpec((blk, A_DV), lambda h, i: (i, h)),
        out_shape=jax.ShapeDtypeStruct((t, d), F32),
        scratch_shapes=[
            pltpu.VMEM((2, 1, blk), F32),
            pltpu.VMEM((2, 1, blk), F32),
            pltpu.VMEM((2, A_DV, blk), F32),
        ],
        compiler_params=_params("parallel", "arbitrary"),
        name="attn_prompt",
    )(qt, kb, vt, lq1, lk1, lq2, lk2, gcol)


def _attn_decode_kernel(pt_ref, q_ref, kn_ref, vn_ref, lq1_ref, lk1_ref, lq2_ref, lk2_ref, g_ref,
                        *rest, n_group, n_tok, lam_init):
    k_refs = rest[:n_group]
    v_refs = rest[n_group:2 * n_group]
    o_ref, qbd_ref, m_ref, l_ref, acc_ref, kpage_ref, vpage_ref = rest[2 * n_group:]
    j = pl.program_id(1)
    rows, d = qbd_ref.shape
    page = kpage_ref.shape[0]
    heads = d // A_DV

    def update(ks, vs, mask):
        q = qbd_ref[...]
        nt = (((1,), (1,)), ((), ()))
        ss = [lax.dot_general(q, k, nt, preferred_element_type=F32) for k in ks]
        if mask is not None:
            ss = [jnp.where(mask, s, NEG_BIG) for s in ss]
        m_old = m_ref[...]
        m_new = m_old
        for s in ss:
            m_new = jnp.maximum(m_new, jnp.max(s, axis=1, keepdims=True))
        alpha = jnp.exp(m_old - m_new)
        l_new = alpha * l_ref[...]
        acc = alpha * acc_ref[...]
        for s, v in zip(ss, vs):
            p = jnp.exp(s - m_new)
            l_new = l_new + jnp.sum(p, axis=1, keepdims=True)
            acc = acc + jnp.dot(p.astype(BF16), v, preferred_element_type=F32)
        m_ref[...] = m_new
        l_ref[...] = l_new
        acc_ref[...] = acc

    @pl.when(j == 0)
    def _():
        q = q_ref[0]
        qrep = jnp.concatenate([q] * (rows // n_tok), axis=0)
        r = lax.broadcasted_iota(jnp.int32, (rows, d), 0)
        c = lax.broadcasted_iota(jnp.int32, (rows, d), 1)
        qbd_ref[...] = jnp.where(r // n_tok == c // A_DQK, qrep, jnp.zeros_like(qrep))
        m_ref[...] = jnp.full(m_ref.shape, NEG_BIG, F32)
        l_ref[...] = jnp.zeros(l_ref.shape, F32)
        acc_ref[...] = jnp.zeros(acc_ref.shape, F32)
        kpage_ref[...] = jnp.zeros(kpage_ref.shape, BF16)
        vpage_ref[...] = jnp.zeros(vpage_ref.shape, BF16)
        kpage_ref[0:n_tok, :] = kn_ref[0].astype(BF16)
        vpage_ref[0:n_tok, :] = vn_ref[0].astype(BF16)
        tq = lax.broadcasted_iota(jnp.int32, (rows, page), 0) % n_tok
        ts = lax.broadcasted_iota(jnp.int32, (rows, page), 1)
        update([kpage_ref[...]], [vpage_ref[...]], ts <= tq)

    def load_page(ref):
        cols = [ref[0, pl.ds(h, page, stride=heads), :] for h in range(heads)]
        return jnp.concatenate(cols, axis=1).astype(BF16)

    update([load_page(k) for k in k_refs], [load_page(v) for v in v_refs], None)

    @pl.when(j == pl.num_programs(1) - 1)
    def _():
        lam = _lambda(lq1_ref[...], lk1_ref[...], lq2_ref[...], lk2_ref[...], lam_init)
        on = acc_ref[...] / l_ref[...]
        g = g_ref[...]
        for h in range(heads):
            r0 = h * 2 * n_tok
            c0 = h * A_DV
            o = on[r0:r0 + n_tok, c0:c0 + A_DV] - lam * on[r0 + n_tok:r0 + 2 * n_tok, c0:c0 + A_DV]
            o_ref[0, :, c0:c0 + A_DV] = _rms(o, g) * (1.0 - lam_init)


def _attn_decode(page_table, q, z3, cache_k, cache_v, lq1, lk1, lq2, lk2, g, *, n_group, lam_init):
    bsz, n_tok, d = q.shape
    n_pages = page_table.shape[1]
    heads = d // A_DV
    page = cache_k.shape[1] // heads
    rows = (d // A_DQK) * n_tok
    vec = pl.BlockSpec((1, A_DQK), lambda b, j, pt: (0, 0))

    def page_spec(g_idx):
        return pl.BlockSpec((1, page * heads, A_DV), lambda b, j, pt: (pt[b, j * n_group + g_idx], 0, 0))

    grid_spec = pltpu.PrefetchScalarGridSpec(
        num_scalar_prefetch=1,
        grid=(bsz, n_pages // n_group),
        in_specs=[
            pl.BlockSpec((1, n_tok, d), lambda b, j, pt: (b, 0, 0)),
            pl.BlockSpec((1, n_tok, d), lambda b, j, pt: (b, 0, 0)),
            pl.BlockSpec((1, n_tok, d), lambda b, j, pt: (b, 0, 1)),
            vec, vec, vec, vec,
            pl.BlockSpec((1, A_DV), lambda b, j, pt: (0, 0)),
        ] + [page_spec(g_idx) for g_idx in range(n_group)] * 2,
        out_specs=pl.BlockSpec((1, n_tok, d), lambda b, j, pt: (b, 0, 0)),
        scratch_shapes=[
            pltpu.VMEM((rows, d), BF16),
            pltpu.VMEM((rows, 1), F32),
            pltpu.VMEM((rows, 1), F32),
            pltpu.VMEM((rows, d), F32),
            pltpu.VMEM((page, d), BF16),
            pltpu.VMEM((page, d), BF16),
        ],
    )
    return pl.pallas_call(
        functools.partial(_attn_decode_kernel, n_group=n_group, n_tok=n_tok, lam_init=lam_init),
        grid_spec=grid_spec,
        out_shape=jax.ShapeDtypeStruct((bsz, n_tok, d), F32),
        compiler_params=_params("parallel", "arbitrary"),
        name="attn_decode",
    )(page_table, q, z3, z3, lq1, lk1, lq2, lk2, g, *([cache_k] * n_group), *([cache_v] * n_group))


def _gla_chunk(q, k, v, la, st, *, sub, scale):
    c, dk = q.shape
    q = q * scale
    t_i = lax.broadcasted_iota(jnp.int32, (c, c), 0)
    s_i = lax.broadcasted_iota(jnp.int32, (c, c), 1)
    tri = (s_i <= t_i).astype(F32)
    b = jnp.dot(tri, la, precision=lax.Precision.HIGHEST, preferred_element_type=F32)
    nt = (((1,), (1,)), ((), ()))

    o = lax.dot_general((q * jnp.exp(b)).astype(BF16), st.astype(BF16), nt, preferred_element_type=F32)

    lane = lax.broadcasted_iota(jnp.int32, (sub, c), 1)
    trow = lax.broadcasted_iota(jnp.int32, (sub, c), 0)
    a_rows = []
    for blk in range(c // sub):
        r0 = blk * sub
        qi = q[r0:r0 + sub]
        bi = b[r0:r0 + sub]
        a_blk = jnp.zeros((sub, c), F32)
        if blk > 0:
            ref = b[r0:r0 + 1]
            qs = qi * jnp.exp(bi - ref)
            ks = k * jnp.exp(jnp.minimum(ref - b, 0.0))
            a_off = lax.dot_general(qs.astype(BF16), ks.astype(BF16), nt, preferred_element_type=F32)
            a_blk = jnp.where(lane < r0, a_off, a_blk)
        for s in range(sub):
            ks_row = k[r0 + s:r0 + s + 1]
            bs_row = b[r0 + s:r0 + s + 1]
            x = qi * ks_row * jnp.exp(jnp.minimum(bi - bs_row, 0.0))
            col = jnp.sum(x, axis=1, keepdims=True)
            a_blk = jnp.where((lane == r0 + s) & (trow >= s), col, a_blk)
        a_rows.append(a_blk)
    a = a_rows[0] if len(a_rows) == 1 else jnp.concatenate(a_rows, axis=0)
    o = o + jnp.dot(a.astype(BF16), v.astype(BF16), preferred_element_type=F32)

    bl = b[c - 1:c]
    kd = k * jnp.exp(bl - b)
    tn = (((0,), (0,)), ((), ()))
    st_new = st * jnp.exp(bl) + lax.dot_general(v.astype(BF16), kd.astype(BF16), tn, preferred_element_type=F32)
    return o, st_new


def _gla_prompt_kernel(q_ref, k_ref, v_ref, la_ref, o_ref, s_ref, st_ref, *, chunk, sub, scale):
    t = pl.program_id(1)

    @pl.when(t == 0)
    def _():
        st_ref[...] = jnp.zeros(st_ref.shape, F32)

    def body(ci, carry):
        r0 = pl.multiple_of(ci * chunk, chunk)
        rows = pl.ds(r0, chunk)
        o, st_new = _gla_chunk(q_ref[rows, :], k_ref[rows, :], v_ref[rows, :], la_ref[rows, :],
                               st_ref[...], sub=sub, scale=scale)
        o_ref[rows, :] = o
        st_ref[...] = st_new
        return carry

    lax.fori_loop(0, q_ref.shape[0] // chunk, body, 0)

    @pl.when(t == pl.num_programs(1) - 1)
    def _():
        s_ref[0] = st_ref[...].T


def _gla_prompt(z, la, *, d, tc, chunk, sub):
    t = z.shape[0]
    dk = d // (2 * B_HEADS)
    dv = d // B_HEADS
    qb0 = 2 * d // dk
    kb0 = qb0 + B_HEADS
    vb0 = 3 * d // dv
    return pl.pallas_call(
        functools.partial(_gla_prompt_kernel, chunk=chunk, sub=sub, scale=dk ** -0.5),
        grid=(B_HEADS, t // tc),
        in_specs=[
            pl.BlockSpec((tc, dk), lambda h, i: (i, qb0 + h)),
            pl.BlockSpec((tc, dk), lambda h, i: (i, kb0 + h)),
            pl.BlockSpec((tc, dv), lambda h, i: (i, vb0 + h)),
            pl.BlockSpec((tc, dk), lambda h, i: (i, h)),
        ],
        out_specs=[
            pl.BlockSpec((tc, dv), lambda h, i: (i, h)),
            pl.BlockSpec((1, dk, dv), lambda h, i: (h, 0, 0)),
        ],
        out_shape=[
            jax.ShapeDtypeStruct((t, d), F32),
            jax.ShapeDtypeStruct((B_HEADS, dk, dv), F32),
        ],
        scratch_shapes=[pltpu.VMEM((dv, dk), F32)],
        compiler_params=_params("parallel", "arbitrary"),
        name="gla_prompt",
    )(z, z, z, la)


def _gla_sample_kernel(q_ref, k_ref, v_ref, la_ref, s0_ref, o_ref, s_ref, *, n_valid, scale):
    c = q_ref.shape[1]
    valid = lax.broadcasted_iota(jnp.int32, (c, 1), 0) < n_valid
    k = jnp.where(valid, k_ref[0], 0.0)
    la = jnp.where(valid, la_ref[0], 0.0)
    o, st_new = _gla_chunk(q_ref[0], k, v_ref[0], la, s0_ref[0, 0].T, sub=c, scale=scale)
    o_ref[0] = o
    s_ref[0, 0] = st_new.T


def _gla_sample(z3, la3, s0, *, d, n_valid):
    bsz, c, _ = z3.shape
    dk = d // (2 * B_HEADS)
    dv = d // B_HEADS
    qb0 = 2 * d // dk
    kb0 = qb0 + B_HEADS
    vb0 = 3 * d // dv
    return pl.pallas_call(
        functools.partial(_gla_sample_kernel, n_valid=n_valid, scale=dk ** -0.5),
        grid=(bsz, B_HEADS),
        in_specs=[
            pl.BlockSpec((1, c, dk), lambda b, h: (b, 0, qb0 + h)),
            pl.BlockSpec((1, c, dk), lambda b, h: (b, 0, kb0 + h)),
            pl.BlockSpec((1, c, dv), lambda b, h: (b, 0, vb0 + h)),
            pl.BlockSpec((1, c, dk), lambda b, h: (b, 0, h)),
            pl.BlockSpec((1, 1, dk, dv), lambda b, h: (b, h, 0, 0)),
        ],
        out_specs=[
            pl.BlockSpec((1, c, dv), lambda b, h: (b, 0, h)),
            pl.BlockSpec((1, 1, dk, dv), lambda b, h: (b, h, 0, 0)),
        ],
        out_shape=[
            jax.ShapeDtypeStruct((bsz, c, d), F32),
            jax.ShapeDtypeStruct(s0.shape, F32),
        ],
        compiler_params=_params("parallel", "parallel"),
        name="gla_sample",
    )(z3, z3, z3, la3, s0)


def _mix_out_kernel(x_ref, oa_ref, ob_ref, r_ref, ga_ref, gb_ref, gsub_ref, wo_ref, gpost_ref, y_ref):
    dv = gsub_ref.shape[1]
    ob = ob_ref[...]
    gsub = gsub_ref[...]
    ob_n = jnp.concatenate(
        [_rms(ob[:, h * dv:(h + 1) * dv], gsub) for h in range(ob.shape[1] // dv)], axis=1)
    ob_n = ob_n * jax.nn.silu(r_ref[...])
    merged = jax.nn.sigmoid(ga_ref[...]) * oa_ref[...] + jax.nn.sigmoid(gb_ref[...]) * ob_n
    y = jnp.dot(merged.astype(BF16), wo_ref[...], preferred_element_type=F32)
    y_ref[...] = x_ref[...] + _rms(y, gpost_ref[...])


def _mix_out(x, oa, ob, z, gsub, wo, gpost, *, tm):
    t, d = x.shape
    row = lambda i: (i, 0)
    const = lambda i: (0, 0)
    return pl.pallas_call(
        _mix_out_kernel,
        grid=(t // tm,),
        in_specs=[
            pl.BlockSpec((tm, d), row),
            pl.BlockSpec((tm, d), row),
            pl.BlockSpec((tm, d), row),
            pl.BlockSpec((tm, d), lambda i: (i, 4)),
            pl.BlockSpec((tm, d), lambda i: (i, 5)),
            pl.BlockSpec((tm, d), lambda i: (i, 6)),
            pl.BlockSpec((1, gsub.shape[1]), const),
            pl.BlockSpec((d, d), const),
            pl.BlockSpec((1, d), const),
        ],
        out_specs=pl.BlockSpec((tm, d), row),
        out_shape=jax.ShapeDtypeStruct((t, d), F32),
        compiler_params=_params("parallel"),
        name="mix_out",
    )(x, oa, ob, z, z, z, gsub, wo, gpost)


def _ffn_kernel(x_ref, gpre_ref, wg_ref, wu_ref, wd_ref, gpost_ref, y_ref, h_ref, acc_ref):
    j = pl.program_id(1)

    @pl.when(j == 0)
    def _():
        h_ref[...] = _rms(x_ref[...], gpre_ref[...]).astype(BF16)
        acc_ref[...] = jnp.zeros(acc_ref.shape, F32)

    h = h_ref[...]
    gate = jnp.dot(h, wg_ref[...], preferred_element_type=F32)
    up = jnp.dot(h, wu_ref[...], preferred_element_type=F32)
    act = (jax.nn.silu(gate) * up).astype(BF16)
    acc_ref[...] += jnp.dot(act, wd_ref[...], preferred_element_type=F32)

    @pl.when(j == pl.num_programs(1) - 1)
    def _():
        y_ref[...] = x_ref[...] + _rms(acc_ref[...], gpost_ref[...])


def _ffn(x, gpre, wg, wu, wd, gpost, *, tm, tf):
    t, d = x.shape
    dff = wg.shape[1]
    row = lambda i, j: (i, 0)
    const = lambda i, j: (0, 0)
    return pl.pallas_call(
        _ffn_kernel,
        grid=(t // tm, dff // tf),
        in_specs=[
            pl.BlockSpec((tm, d), row),
            pl.BlockSpec((1, d), const),
            pl.BlockSpec((d, tf), lambda i, j: (0, j)),
            pl.BlockSpec((d, tf), lambda i, j: (0, j)),
            pl.BlockSpec((tf, d), lambda i, j: (j, 0)),
            pl.BlockSpec((1, d), const),
        ],
        out_specs=pl.BlockSpec((tm, d), row),
        out_shape=jax.ShapeDtypeStruct((t, d), F32),
        scratch_shapes=[pltpu.VMEM((tm, d), BF16), pltpu.VMEM((tm, d), F32)],
        compiler_params=_params("parallel", "arbitrary"),
        name="ffn",
    )(x, gpre, wg, wu, wd, gpost)


def _largest_divisor(n, candidates):
    for c in candidates:
        if n % c == 0:
            return c
    raise ValueError(f"no supported tile for extent {n}")


def _prep_weights(g_mix_pre, w_in, w_gk, b_gk, g_attn_sub, g_gla_sub, w_out, g_mix_post,
                  g_ffn_pre, w_gate, w_up, w_down, g_ffn_post):
    d = w_in.shape[0]
    rank = w_gk.shape[0]
    a_w, wk, wv = d, d // 2, d
    off = [0]
    for s in (a_w, a_w, a_w, wk, wk, wv, rank, wv, d, d):
        off.append(off[-1] + s)
    col = lambda a, b: w_in[:, off[a]:off[b]]
    wq_t = col(0, 1).T.astype(BF16)
    wv_t = col(2, 3).T.astype(BF16)
    wm = jnp.concatenate([col(1, 2), col(2, 3), col(3, 5), col(5, 6), col(7, 8), col(8, 9), col(9, 10)],
                         axis=1).astype(BF16)
    wg = jnp.pad(col(6, 7), ((0, 0), (0, LANES - rank))).astype(BF16)
    wgk = jnp.pad(w_gk, ((0, LANES - rank), (0, 0))).astype(BF16)
    row = lambda v: v.reshape(1, -1).astype(F32)
    return dict(
        g_mix_pre=row(g_mix_pre), wq_t=wq_t, wv_t=wv_t, wm=wm, wg=wg, wgk=wgk, bgk=row(b_gk),
        g_attn_row=row(g_attn_sub), g_attn_col=g_attn_sub.reshape(-1, 1).astype(F32),
        g_gla_sub=row(g_gla_sub), w_out=w_out.astype(BF16), g_mix_post=row(g_mix_post),
        g_ffn_pre=row(g_ffn_pre), w_gate=w_gate.astype(BF16), w_up=w_up.astype(BF16),
        w_down=w_down.astype(BF16), g_ffn_post=row(g_ffn_post))


def _trunk_tail(x2, oa, ob, z, w):
    t = x2.shape[0]
    tm = _largest_divisor(t, (512, 256, 128, 64, 32, 16, 8))
    x1 = _mix_out(x2, oa, ob, z, w["g_gla_sub"], w["w_out"], w["g_mix_post"], tm=tm)
    dff = w["w_gate"].shape[1]
    tf = dff // 2 if (dff // 2) % LANES == 0 else dff
    tmf = _largest_divisor(t, (1024, 512, 256, 128, 64, 32, 16, 8))
    return _ffn(x1, w["g_ffn_pre"], w["w_gate"], w["w_up"], w["w_down"], w["g_ffn_post"], tm=tmf, tf=tf)


def kernel(x_prompt, x_sample, cache_k, cache_v, state_gla, page_table, g_mix_pre, w_in, w_gk, b_gk,
           lambda_q1, lambda_k1, lambda_q2, lambda_k2, g_attn_sub, g_gla_sub, w_out, g_mix_post,
           g_ffn_pre, w_gate, w_up, w_down, g_ffn_post):
    depth = w_in.shape[0]
    bsz, seq, d = x_prompt.shape
    dbsz, dseq, _ = x_sample.shape
    heads = d // A_DV
    page = cache_k.shape[2]
    n_pool = cache_k.shape[1]
    n_pages = page_table.shape[1]
    dk = d // (2 * B_HEADS)
    dv = d // B_HEADS
    assert bsz == 1, "prompt kernels take one sequence"
    pad_seq = -(-dseq // SUBLANES) * SUBLANES

    x_p = x_prompt.reshape(seq, d)
    x_s = jnp.pad(x_sample, ((0, 0), (0, pad_seq - dseq), (0, 0))).reshape(dbsz * pad_seq, d)

    outs = ([], [], [], [], [], [])
    for l in range(depth):
        lam_init = 0.8 - 0.6 * math.exp(-0.3 * l)
        w = _prep_weights(g_mix_pre[l], w_in[l], w_gk[l], b_gk[l], g_attn_sub[l], g_gla_sub[l], w_out[l],
                          g_mix_post[l], g_ffn_pre[l], w_gate[l], w_up[l], w_down[l], g_ffn_post[l])
        lams = [v[l].reshape(1, A_DQK).astype(F32) for v in (lambda_q1, lambda_k1, lambda_q2, lambda_k2)]
        proj = functools.partial(_in_proj, g=w["g_mix_pre"], wq_t=w["wq_t"], wv_t=w["wv_t"], wg=w["wg"],
                                 wgk=w["wgk"], bgk=w["bgk"], wm=w["wm"])

        tm = _largest_divisor(seq, (512, 256, 128))
        z, qt, vt, kb, la = proj(x_p, tm=tm)
        blk = _largest_divisor(seq, (512, 256, 128))
        oa = _attn_prompt(qt, kb, vt, *lams, w["g_attn_col"], blk=blk, lam_init=lam_init)
        chunk = _largest_divisor(seq, (128, 64, 32, 16, 8))
        tc = _largest_divisor(seq, (1024, 512, 256, 128, 64, 32, 16, 8))
        ob, s_p = _gla_prompt(z, la, d=d, tc=tc, chunk=chunk, sub=min(16, chunk))
        x_p = _trunk_tail(x_p, oa, ob, z, w)
        outs[0].append(z[:, 0:d].reshape(bsz, seq, heads, A_DV))
        outs[1].append(z[:, d:2 * d].reshape(bsz, seq, heads, A_DV))
        outs[2].append(s_p.reshape(bsz, B_HEADS, dk, dv))

        ts = dbsz * pad_seq
        zs, qts, _, _, las = proj(x_s, tm=_largest_divisor(ts, (256, 128)))
        zs3 = zs.reshape(dbsz, pad_seq, zs.shape[1])
        q_s = qts.T.reshape(dbsz, pad_seq, d)
        n_group = _largest_divisor(n_pages, (8, 4, 2, 1))
        oas = _attn_decode(page_table, q_s, zs3, cache_k[l].reshape(n_pool, page * heads, A_DV),
                           cache_v[l].reshape(n_pool, page * heads, A_DV), *lams, w["g_attn_row"],
                           n_group=n_group, lam_init=lam_init)
        obs, s_s = _gla_sample(zs3, las.reshape(dbsz, pad_seq, las.shape[1]), state_gla[l], d=d, n_valid=dseq)
        x_s = _trunk_tail(x_s, oas.reshape(ts, d), obs.reshape(ts, d), zs, w)
        outs[3].append(zs3[:, :dseq, 0:d].reshape(dbsz, dseq, heads, A_DV))
        outs[4].append(zs3[:, :dseq, d:2 * d].reshape(dbsz, dseq, heads, A_DV))
        outs[5].append(s_s)

    y_p = x_p.reshape(bsz, seq, d)
    y_s = x_s.reshape(dbsz, pad_seq, d)[:, :dseq]
    k_p, v_p, g_p, k_s, v_s, g_s = (jnp.stack(o) for o in outs)
    return (y_p, y_s, k_p, v_p, g_p, k_s, v_s, g_s)
```

```python
import functools
import math

import jax
import jax.numpy as jnp
from jax import lax
from jax.experimental import pallas as pl
from jax.experimental.pallas import tpu as pltpu

A_DQK = 64
A_DV = 2 * A_DQK
B_HEADS = 4
GATE_TAU = 16.0
EPS = 1e-6

LANES = 128
SUBLANES = 8
VMEM_LIMIT_BYTES = 56 * 1024 * 1024

NEG_BIG = -1e30
LOG2_E = math.log2(math.e)
ATTN_SCORE_BUFFERS = 3
F32 = jnp.float32
BF16 = jnp.bfloat16


def _params(*semantics):
    return pltpu.CompilerParams(dimension_semantics=semantics, vmem_limit_bytes=VMEM_LIMIT_BYTES)


def _rms(x, g):
    return x * lax.rsqrt(jnp.mean(x * x, axis=-1, keepdims=True) + EPS) * g


def _lambda(lq1, lk1, lq2, lk2, lam_init):
    s1 = jnp.sum(lq1 * lk1, axis=-1, keepdims=True)
    s2 = jnp.sum(lq2 * lk2, axis=-1, keepdims=True)
    return jnp.exp(s1) - jnp.exp(s2) + lam_init


def _in_proj_kernel(x_ref, g_ref, wq_t_ref, wv_t_ref, wg_ref, wgk_ref, bgk_ref, wm_ref,
                    z_ref, qt_ref, vt_ref, kb_ref, la_ref, h_ref, *, q_scale):
    j = pl.program_id(1)

    @pl.when(j == 0)
    def _():
        h = _rms(x_ref[...], g_ref[...]).astype(BF16)
        h_ref[...] = h
        nt = (((1,), (1,)), ((), ()))
        qt = lax.dot_general(wq_t_ref[...], h, nt, preferred_element_type=F32)
        qt_ref[...] = (qt * q_scale).astype(BF16)
        vt = lax.dot_general(wv_t_ref[...], h, nt, preferred_element_type=F32)
        vt_ref[...] = vt.astype(BF16)
        g_lr = jnp.dot(h, wg_ref[...], preferred_element_type=F32)
        gk = jnp.dot(g_lr.astype(BF16), wgk_ref[...], preferred_element_type=F32) + bgk_ref[...]
        la_ref[...] = jax.nn.log_sigmoid(gk) / GATE_TAU

    @pl.when(j > 0)
    def _():
        z = jnp.dot(h_ref[...], wm_ref[...], preferred_element_type=F32)
        z_ref[...] = z

        @pl.when(j == 1)
        def _():
            kb_ref[...] = z.astype(BF16)


def _in_proj(x, g, wq_t, wv_t, wg, wgk, bgk, wm, *, tm):
    t, d = x.shape
    n_tiles = wm.shape[1] // d
    wk = wgk.shape[1]
    grid = (t // tm, n_tiles + 1)
    const = lambda i, j: (0, 0)
    return pl.pallas_call(
        functools.partial(_in_proj_kernel, q_scale=A_DQK ** -0.5 * LOG2_E),
        grid=grid,
        in_specs=[
            pl.BlockSpec((tm, d), lambda i, j: (i, 0)),
            pl.BlockSpec((1, d), const),
            pl.BlockSpec((d, d), const),
            pl.BlockSpec((d, d), const),
            pl.BlockSpec((d, LANES), const),
            pl.BlockSpec((LANES, wk), const),
            pl.BlockSpec((1, wk), const),
            pl.BlockSpec((d, d), lambda i, j: (0, jnp.maximum(j - 1, 0))),
        ],
        out_specs=[
            pl.BlockSpec((tm, d), lambda i, j: (i, jnp.maximum(j - 1, 0))),
            pl.BlockSpec((d, tm), lambda i, j: (0, i)),
            pl.BlockSpec((d, tm), lambda i, j: (0, i)),
            pl.BlockSpec((tm, d), lambda i, j: (i, 0)),
            pl.BlockSpec((tm, wk), lambda i, j: (i, 0)),
        ],
        out_shape=[
            jax.ShapeDtypeStruct((t, n_tiles * d), F32),
            jax.ShapeDtypeStruct((d, t), BF16),
            jax.ShapeDtypeStruct((d, t), BF16),
            jax.ShapeDtypeStruct((t, d), BF16),
            jax.ShapeDtypeStruct((t, wk), F32),
        ],
        scratch_shapes=[pltpu.VMEM((tm, d), BF16)],
        compiler_params=_params("parallel", "arbitrary"),
        name="in_proj",
    )(x, g, wq_t, wv_t, wg, wgk, bgk, wm)


def _attn_prompt_kernel(qt_ref, k_ref, vt_ref, lq1_ref, lk1_ref, lq2_ref, lk2_ref, gcol_ref,
                        o_ref, qs_ref, s_refs, m_ref, l_ref, acc_ref, *, blk, lam_init):
    n_buf = len(s_refs)
    i = pl.program_id(1)
    q = qt_ref[...]
    row = lax.broadcasted_iota(jnp.int32, q.shape, 0)
    zero = jnp.zeros_like(q)
    qs_ref[0] = jnp.where(row < A_DQK, q, zero)
    qs_ref[1] = jnp.where(row >= A_DQK, q, zero)

    m_ref[...] = jnp.full(m_ref.shape, NEG_BIG, F32)
    l_ref[...] = jnp.zeros(l_ref.shape, F32)
    acc_ref[...] = jnp.zeros(acc_ref.shape, F32)

    def scores(kb, slot):
        start = pl.multiple_of(kb * blk, blk)
        k = k_ref[pl.ds(start, blk), :]
        for mi in range(2):
            s_refs[slot][mi] = jnp.dot(k, qs_ref[mi], preferred_element_type=F32)

    def absorb(kb, slot, masked):
        start = pl.multiple_of(kb * blk, blk)
        vt = vt_ref[:, pl.ds(start, blk)]
        for mi in range(2):
            s = s_refs[slot][mi]
            if masked:
                kpos = lax.broadcasted_iota(jnp.int32, s.shape, 0)
                qpos = lax.broadcasted_iota(jnp.int32, s.shape, 1)
                s = jnp.where(kpos <= qpos, s, NEG_BIG)
            m_old = m_ref[mi]
            m_new = jnp.maximum(m_old, jnp.max(s, axis=0, keepdims=True))
            alpha = jnp.exp2(m_old - m_new)
            p = jnp.exp2(s - m_new)
            l_ref[mi] = alpha * l_ref[mi] + jnp.sum(p, axis=0, keepdims=True)
            acc_ref[mi] = alpha * acc_ref[mi] + jnp.dot(vt, p.astype(BF16), preferred_element_type=F32)
            m_ref[mi] = m_new

    ahead = n_buf - 1
    scores(0, 0)
    for j in range(1, ahead):
        pl.when(j <= i)(functools.partial(scores, j, j))

    n_full = jnp.maximum(i - n_buf + 2, 0) // n_buf

    def full_round(r, carry):
        kb = r * n_buf
        for j in range(n_buf):
            scores(kb + j + ahead, (j + ahead) % n_buf)
            absorb(kb + j, j, False)
        return carry

    lax.fori_loop(0, n_full, full_round, 0)

    kb0 = n_full * n_buf
    rest = i - kb0 + 1
    for count in range(1, 2 * n_buf - 1):
        def tail(count=count):
            for j in range(count):
                if ahead <= j + ahead < count:
                    scores(kb0 + j + ahead, (j + ahead) % n_buf)
                absorb(kb0 + j, j % n_buf, j == count - 1)
        pl.when(rest == count)(tail)

    lam = _lambda(lq1_ref[...], lk1_ref[...], lq2_ref[...], lk2_ref[...], lam_init)
    o = acc_ref[0] / l_ref[0] - lam * (acc_ref[1] / l_ref[1])
    ms = jnp.mean(o * o, axis=0, keepdims=True)
    y = o * lax.rsqrt(ms + EPS) * gcol_ref[...] * (1.0 - lam_init)
    o_ref[...] = y.T


def _attn_prompt(qt, kb, vt, lq1, lk1, lq2, lk2, gcol, *, blk, lam_init):
    d, t = qt.shape
    heads = d // A_DV
    vec = pl.BlockSpec((1, A_DQK), lambda h, i: (0, 0))
    return pl.pallas_call(
        functools.partial(_attn_prompt_kernel, blk=blk, lam_init=lam_init),
        grid=(heads, t // blk),
        in_specs=[
            pl.BlockSpec((A_DV, blk), lambda h, i: (h, i)),
            pl.BlockSpec((t, A_DV), lambda h, i: (0, h)),
            pl.BlockSpec((A_DV, t), lambda h, i: (h, 0)),
            vec, vec, vec, vec,
            pl.BlockSpec((A_DV, 1), lambda h, i: (0, 0)),
        ],
        out_specs=pl.BlockSpec((blk, A_DV), lambda h, i: (i, h)),
        out_shape=jax.ShapeDtypeStruct((t, d), F32),
        scratch_shapes=[
            pltpu.VMEM((2, A_DV, blk), BF16),
            tuple(pltpu.VMEM((2, blk, blk), F32) for _ in range(ATTN_SCORE_BUFFERS)),
            pltpu.VMEM((2, 1, blk), F32),
            pltpu.VMEM((2, 1, blk), F32),
            pltpu.VMEM((2, A_DV, blk), F32),
        ],
        compiler_params=_params("parallel", "arbitrary"),
        name="attn_prompt",
    )(qt, kb, vt, lq1, lk1, lq2, lk2, gcol)


def _attn_decode_kernel(pt_ref, q_ref, kn_ref, vn_ref, lq1_ref, lk1_ref, lq2_ref, lk2_ref, g_ref,
                        *rest, n_group, n_tok, lam_init):
    k_refs = rest[:n_group]
    v_refs = rest[n_group:2 * n_group]
    o_ref, qbd_ref, m_ref, l_ref, acc_ref, kpage_ref, vpage_ref, kcat_ref, vcat_ref = rest[2 * n_group:]
    j = pl.program_id(1)
    rows, d = qbd_ref.shape
    page = kpage_ref.shape[0]
    heads = d // A_DV

    def update(k, v, mask):
        nt = (((1,), (1,)), ((), ()))
        s = lax.dot_general(qbd_ref[...], k, nt, preferred_element_type=F32)
        if mask is not None:
            s = jnp.where(mask, s, NEG_BIG)
        m_old = m_ref[...]
        m_new = jnp.maximum(m_old, jnp.max(s, axis=1, keepdims=True))
        alpha = jnp.exp2(m_old - m_new)
        p = jnp.exp2(s - m_new)
        l_ref[...] = alpha * l_ref[...] + jnp.sum(p, axis=1, keepdims=True)
        acc_ref[...] = alpha * acc_ref[...] + jnp.dot(p.astype(BF16), v, preferred_element_type=F32)
        m_ref[...] = m_new

    @pl.when(j == 0)
    def _():
        q = q_ref[0]
        qrep = jnp.concatenate([q] * (rows // n_tok), axis=0)
        r = lax.broadcasted_iota(jnp.int32, (rows, d), 0)
        c = lax.broadcasted_iota(jnp.int32, (rows, d), 1)
        qbd_ref[...] = jnp.where(r // n_tok == c // A_DQK, qrep, jnp.zeros_like(qrep))
        m_ref[...] = jnp.full(m_ref.shape, NEG_BIG, F32)
        l_ref[...] = jnp.zeros(l_ref.shape, F32)
        acc_ref[...] = jnp.zeros(acc_ref.shape, F32)
        kpage_ref[...] = jnp.zeros(kpage_ref.shape, BF16)
        vpage_ref[...] = jnp.zeros(vpage_ref.shape, BF16)
        kpage_ref[0:n_tok, :] = kn_ref[0].astype(BF16)
        vpage_ref[0:n_tok, :] = vn_ref[0].astype(BF16)
        tq = lax.broadcasted_iota(jnp.int32, (rows, page), 0) % n_tok
        ts = lax.broadcasted_iota(jnp.int32, (rows, page), 1)
        update(kpage_ref[...], vpage_ref[...], ts <= tq)

    def load_pages(refs, dst_ref):
        for g_idx, ref in enumerate(refs):
            cols = [ref[0, pl.ds(h, page, stride=heads), :] for h in range(heads)]
            dst_ref[g_idx * page:(g_idx + 1) * page, :] = jnp.concatenate(cols, axis=1).astype(BF16)
        return dst_ref[...]

    update(load_pages(k_refs, kcat_ref), load_pages(v_refs, vcat_ref), None)

    @pl.when(j == pl.num_programs(1) - 1)
    def _():
        lam = _lambda(lq1_ref[...], lk1_ref[...], lq2_ref[...], lk2_ref[...], lam_init)
        on = acc_ref[...] / l_ref[...]
        g = g_ref[...]
        for h in range(heads):
            r0 = h * 2 * n_tok
            c0 = h * A_DV
            o = on[r0:r0 + n_tok, c0:c0 + A_DV] - lam * on[r0 + n_tok:r0 + 2 * n_tok, c0:c0 + A_DV]
            o_ref[0, :, c0:c0 + A_DV] = _rms(o, g) * (1.0 - lam_init)


def _attn_decode(page_table, q, z3, cache_k, cache_v, lq1, lk1, lq2, lk2, g, *, n_group, lam_init):
    bsz, n_tok, d = q.shape
    n_pages = page_table.shape[1]
    heads = d // A_DV
    page = cache_k.shape[1] // heads
    rows = (d // A_DQK) * n_tok
    vec = pl.BlockSpec((1, A_DQK), lambda b, j, pt: (0, 0))

    def page_spec(g_idx):
        return pl.BlockSpec((1, page * heads, A_DV), lambda b, j, pt: (pt[b, j * n_group + g_idx], 0, 0))

    grid_spec = pltpu.PrefetchScalarGridSpec(
        num_scalar_prefetch=1,
        grid=(bsz, n_pages // n_group),
        in_specs=[
            pl.BlockSpec((1, n_tok, d), lambda b, j, pt: (b, 0, 0)),
            pl.BlockSpec((1, n_tok, d), lambda b, j, pt: (b, 0, 0)),
            pl.BlockSpec((1, n_tok, d), lambda b, j, pt: (b, 0, 1)),
            vec, vec, vec, vec,
            pl.BlockSpec((1, A_DV), lambda b, j, pt: (0, 0)),
        ] + [page_spec(g_idx) for g_idx in range(n_group)] * 2,
        out_specs=pl.BlockSpec((1, n_tok, d), lambda b, j, pt: (b, 0, 0)),
        scratch_shapes=[
            pltpu.VMEM((rows, d), BF16),
            pltpu.VMEM((rows, 1), F32),
            pltpu.VMEM((rows, 1), F32),
            pltpu.VMEM((rows, d), F32),
            pltpu.VMEM((page, d), BF16),
            pltpu.VMEM((page, d), BF16),
            pltpu.VMEM((n_group * page, d), BF16),
            pltpu.VMEM((n_group * page, d), BF16),
        ],
    )
    return pl.pallas_call(
        functools.partial(_attn_decode_kernel, n_group=n_group, n_tok=n_tok, lam_init=lam_init),
        grid_spec=grid_spec,
        out_shape=jax.ShapeDtypeStruct((bsz, n_tok, d), F32),
        compiler_params=_params("parallel", "arbitrary"),
        name="attn_decode",
    )(page_table, q, z3, z3, lq1, lk1, lq2, lk2, g, *([cache_k] * n_group), *([cache_v] * n_group))


def _gla_chunk(q, k, v, la, st, *, sub, scale):
    c, dk = q.shape
    q = q * scale
    t_i = lax.broadcasted_iota(jnp.int32, (c, c), 0)
    s_i = lax.broadcasted_iota(jnp.int32, (c, c), 1)
    tri = (s_i <= t_i).astype(F32)
    b = jnp.dot(tri, la, precision=lax.Precision.HIGHEST, preferred_element_type=F32)
    nt = (((1,), (1,)), ((), ()))

    o = lax.dot_general((q * jnp.exp(b)).astype(BF16), st.astype(BF16), nt, preferred_element_type=F32)

    lane = lax.broadcasted_iota(jnp.int32, (sub, c), 1)
    trow = lax.broadcasted_iota(jnp.int32, (sub, c), 0)
    a_rows = []
    for blk in range(c // sub):
        r0 = blk * sub
        qi = q[r0:r0 + sub]
        bi = b[r0:r0 + sub]
        a_blk = jnp.zeros((sub, c), F32)
        if blk > 0:
            ref = b[r0:r0 + 1]
            qs = qi * jnp.exp(bi - ref)
            ks = k * jnp.exp(jnp.minimum(ref - b, 0.0))
            a_off = lax.dot_general(qs.astype(BF16), ks.astype(BF16), nt, preferred_element_type=F32)
            a_blk = jnp.where(lane < r0, a_off, a_blk)
        for s in range(sub):
            ks_row = k[r0 + s:r0 + s + 1]
            bs_row = b[r0 + s:r0 + s + 1]
            x = qi * ks_row * jnp.exp(jnp.minimum(bi - bs_row, 0.0))
            col = jnp.sum(x, axis=1, keepdims=True)
            a_blk = jnp.where((lane == r0 + s) & (trow >= s), col, a_blk)
        a_rows.append(a_blk)
    a = a_rows[0] if len(a_rows) == 1 else jnp.concatenate(a_rows, axis=0)
    o = o + jnp.dot(a.astype(BF16), v.astype(BF16), preferred_element_type=F32)

    bl = b[c - 1:c]
    kd = k * jnp.exp(bl - b)
    tn = (((0,), (0,)), ((), ()))
    st_new = st * jnp.exp(bl) + lax.dot_general(v.astype(BF16), kd.astype(BF16), tn, preferred_element_type=F32)
    return o, st_new


def _gla_prompt_kernel(q_ref, k_ref, v_ref, la_ref, o_ref, s_ref, st_ref, *, chunk, sub, scale):
    t = pl.program_id(1)

    @pl.when(t == 0)
    def _():
        st_ref[...] = jnp.zeros(st_ref.shape, F32)

    def body(ci, carry):
        r0 = pl.multiple_of(ci * chunk, chunk)
        rows = pl.ds(r0, chunk)
        o, st_new = _gla_chunk(q_ref[rows, :], k_ref[rows, :], v_ref[rows, :], la_ref[rows, :],
                               st_ref[...], sub=sub, scale=scale)
        o_ref[rows, :] = o
        st_ref[...] = st_new
        return carry

    lax.fori_loop(0, q_ref.shape[0] // chunk, body, 0)

    @pl.when(t == pl.num_programs(1) - 1)
    def _():
        s_ref[0] = st_ref[...].T


def _gla_prompt(z, la, *, d, tc, chunk, sub):
    t = z.shape[0]
    dk = d // (2 * B_HEADS)
    dv = d // B_HEADS
    qb0 = 2 * d // dk
    kb0 = qb0 + B_HEADS
    vb0 = 3 * d // dv
    return pl.pallas_call(
        functools.partial(_gla_prompt_kernel, chunk=chunk, sub=sub, scale=dk ** -0.5),
        grid=(B_HEADS, t // tc),
        in_specs=[
            pl.BlockSpec((tc, dk), lambda h, i: (i, qb0 + h)),
            pl.BlockSpec((tc, dk), lambda h, i: (i, kb0 + h)),
            pl.BlockSpec((tc, dv), lambda h, i: (i, vb0 + h)),
            pl.BlockSpec((tc, dk), lambda h, i: (i, h)),
        ],
        out_specs=[
            pl.BlockSpec((tc, dv), lambda h, i: (i, h)),
            pl.BlockSpec((1, dk, dv), lambda h, i: (h, 0, 0)),
        ],
        out_shape=[
            jax.ShapeDtypeStruct((t, d), F32),
            jax.ShapeDtypeStruct((B_HEADS, dk, dv), F32),
        ],
        scratch_shapes=[pltpu.VMEM((dv, dk), F32)],
        compiler_params=_params("parallel", "arbitrary"),
        name="gla_prompt",
    )(z, z, z, la)


def _gla_sample_kernel(q_ref, k_ref, v_ref, la_ref, s0_ref, o_ref, s_ref, *, n_valid, scale):
    c = q_ref.shape[1]
    valid = lax.broadcasted_iota(jnp.int32, (c, 1), 0) < n_valid
    k = jnp.where(valid, k_ref[0], 0.0)
    la = jnp.where(valid, la_ref[0], 0.0)
    o, st_new = _gla_chunk(q_ref[0], k, v_ref[0], la, s0_ref[0, 0].T, sub=c, scale=scale)
    o_ref[0] = o
    s_ref[0, 0] = st_new.T


def _gla_sample(z3, la3, s0, *, d, n_valid):
    bsz, c, _ = z3.shape
    dk = d // (2 * B_HEADS)
    dv = d // B_HEADS
    qb0 = 2 * d // dk
    kb0 = qb0 + B_HEADS
    vb0 = 3 * d // dv
    return pl.pallas_call(
        functools.partial(_gla_sample_kernel, n_valid=n_valid, scale=dk ** -0.5),
        grid=(bsz, B_HEADS),
        in_specs=[
            pl.BlockSpec((1, c, dk), lambda b, h: (b, 0, qb0 + h)),
            pl.BlockSpec((1, c, dk), lambda b, h: (b, 0, kb0 + h)),
            pl.BlockSpec((1, c, dv), lambda b, h: (b, 0, vb0 + h)),
            pl.BlockSpec((1, c, dk), lambda b, h: (b, 0, h)),
            pl.BlockSpec((1, 1, dk, dv), lambda b, h: (b, h, 0, 0)),
        ],
        out_specs=[
            pl.BlockSpec((1, c, dv), lambda b, h: (b, 0, h)),
            pl.BlockSpec((1, 1, dk, dv), lambda b, h: (b, h, 0, 0)),
        ],
        out_shape=[
            jax.ShapeDtypeStruct((bsz, c, d), F32),
            jax.ShapeDtypeStruct(s0.shape, F32),
        ],
        compiler_params=_params("parallel", "parallel"),
        name="gla_sample",
    )(z3, z3, z3, la3, s0)


def _mix_out_kernel(x_ref, oa_ref, ob_ref, r_ref, ga_ref, gb_ref, gsub_ref, wo_ref, gpost_ref, y_ref):
    dv = gsub_ref.shape[1]
    ob = ob_ref[...]
    gsub = gsub_ref[...]
    ob_n = jnp.concatenate(
        [_rms(ob[:, h * dv:(h + 1) * dv], gsub) for h in range(ob.shape[1] // dv)], axis=1)
    ob_n = ob_n * jax.nn.silu(r_ref[...])
    merged = jax.nn.sigmoid(ga_ref[...]) * oa_ref[...] + jax.nn.sigmoid(gb_ref[...]) * ob_n
    y = jnp.dot(merged.astype(BF16), wo_ref[...], preferred_element_type=F32)
    y_ref[...] = x_ref[...] + _rms(y, gpost_ref[...])


def _mix_out(x, oa, ob, z, gsub, wo, gpost, *, tm):
    t, d = x.shape
    row = lambda i: (i, 0)
    const = lambda i: (0, 0)
    return pl.pallas_call(
        _mix_out_kernel,
        grid=(t // tm,),
        in_specs=[
            pl.BlockSpec((tm, d), row),
            pl.BlockSpec((tm, d), row),
            pl.BlockSpec((tm, d), row),
            pl.BlockSpec((tm, d), lambda i: (i, 4)),
            pl.BlockSpec((tm, d), lambda i: (i, 5)),
            pl.BlockSpec((tm, d), lambda i: (i, 6)),
            pl.BlockSpec((1, gsub.shape[1]), const),
            pl.BlockSpec((d, d), const),
            pl.BlockSpec((1, d), const),
        ],
        out_specs=pl.BlockSpec((tm, d), row),
        out_shape=jax.ShapeDtypeStruct((t, d), F32),
        compiler_params=_params("parallel"),
        name="mix_out",
    )(x, oa, ob, z, z, z, gsub, wo, gpost)


def _ffn_kernel(x_ref, gpre_ref, wg_ref, wu_ref, wd_ref, gpost_ref, y_ref, h_ref, acc_ref):
    j = pl.program_id(1)

    @pl.when(j == 0)
    def _():
        h_ref[...] = _rms(x_ref[...], gpre_ref[...]).astype(BF16)
        acc_ref[...] = jnp.zeros(acc_ref.shape, F32)

    h = h_ref[...]
    gate = jnp.dot(h, wg_ref[...], preferred_element_type=F32)
    up = jnp.dot(h, wu_ref[...], preferred_element_type=F32)
    act = (jax.nn.silu(gate) * up).astype(BF16)
    acc_ref[...] += jnp.dot(act, wd_ref[...], preferred_element_type=F32)

    @pl.when(j == pl.num_programs(1) - 1)
    def _():
        y_ref[...] = x_ref[...] + _rms(acc_ref[...], gpost_ref[...])


def _ffn(x, gpre, wg, wu, wd, gpost, *, tm, tf):
    t, d = x.shape
    dff = wg.shape[1]
    row = lambda i, j: (i, 0)
    const = lambda i, j: (0, 0)
    return pl.pallas_call(
        _ffn_kernel,
        grid=(t // tm, dff // tf),
        in_specs=[
            pl.BlockSpec((tm, d), row),
            pl.BlockSpec((1, d), const),
            pl.BlockSpec((d, tf), lambda i, j: (0, j)),
            pl.BlockSpec((d, tf), lambda i, j: (0, j)),
            pl.BlockSpec((tf, d), lambda i, j: (j, 0)),
            pl.BlockSpec((1, d), const),
        ],
        out_specs=pl.BlockSpec((tm, d), row),
        out_shape=jax.ShapeDtypeStruct((t, d), F32),
        scratch_shapes=[pltpu.VMEM((tm, d), BF16), pltpu.VMEM((tm, d), F32)],
        compiler_params=_params("parallel", "arbitrary"),
        name="ffn",
    )(x, gpre, wg, wu, wd, gpost)


def _largest_divisor(n, candidates):
    for c in candidates:
        if n % c == 0:
            return c
    raise ValueError(f"no supported tile for extent {n}")


def _prep_weights(g_mix_pre, w_in, w_gk, b_gk, g_attn_sub, g_gla_sub, w_out, g_mix_post,
                  g_ffn_pre, w_gate, w_up, w_down, g_ffn_post):
    d = w_in.shape[0]
    rank = w_gk.shape[0]
    a_w, wk, wv = d, d // 2, d
    off = [0]
    for s in (a_w, a_w, a_w, wk, wk, wv, rank, wv, d, d):
        off.append(off[-1] + s)
    col = lambda a, b: w_in[:, off[a]:off[b]]
    wq_t = col(0, 1).T.astype(BF16)
    wv_t = col(2, 3).T.astype(BF16)
    wm = jnp.concatenate([col(1, 2), col(2, 3), col(3, 5), col(5, 6), col(7, 8), col(8, 9), col(9, 10)],
                         axis=1).astype(BF16)
    wg = jnp.pad(col(6, 7), ((0, 0), (0, LANES - rank))).astype(BF16)
    wgk = jnp.pad(w_gk, ((0, LANES - rank), (0, 0))).astype(BF16)
    row = lambda v: v.reshape(1, -1).astype(F32)
    return dict(
        g_mix_pre=row(g_mix_pre), wq_t=wq_t, wv_t=wv_t, wm=wm, wg=wg, wgk=wgk, bgk=row(b_gk),
        g_attn_row=row(g_attn_sub), g_attn_col=g_attn_sub.reshape(-1, 1).astype(F32),
        g_gla_sub=row(g_gla_sub), w_out=w_out.astype(BF16), g_mix_post=row(g_mix_post),
        g_ffn_pre=row(g_ffn_pre), w_gate=w_gate.astype(BF16), w_up=w_up.astype(BF16),
        w_down=w_down.astype(BF16), g_ffn_post=row(g_ffn_post))


def _trunk_tail(x2, oa, ob, z, w):
    t = x2.shape[0]
    tm = _largest_divisor(t, (512, 256, 128, 64, 32, 16, 8))
    x1 = _mix_out(x2, oa, ob, z, w["g_gla_sub"], w["w_out"], w["g_mix_post"], tm=tm)
    dff = w["w_gate"].shape[1]
    tf = dff // 2 if (dff // 2) % LANES == 0 else dff
    tmf = _largest_divisor(t, (1024, 512, 256, 128, 64, 32, 16, 8))
    return _ffn(x1, w["g_ffn_pre"], w["w_gate"], w["w_up"], w["w_down"], w["g_ffn_post"], tm=tmf, tf=tf)


def kernel(x_prompt, x_sample, cache_k, cache_v, state_gla, page_table, g_mix_pre, w_in, w_gk, b_gk,
           lambda_q1, lambda_k1, lambda_q2, lambda_k2, g_attn_sub, g_gla_sub, w_out, g_mix_post,
           g_ffn_pre, w_gate, w_up, w_down, g_ffn_post):
    depth = w_in.shape[0]
    bsz, seq, d = x_prompt.shape
    dbsz, dseq, _ = x_sample.shape
    heads = d // A_DV
    page = cache_k.shape[2]
    n_pool = cache_k.shape[1]
    n_pages = page_table.shape[1]
    dk = d // (2 * B_HEADS)
    dv = d // B_HEADS
    assert bsz == 1, "prompt kernels take one sequence"
    pad_seq = -(-dseq // SUBLANES) * SUBLANES

    x_p = x_prompt.reshape(seq, d)
    x_s = jnp.pad(x_sample, ((0, 0), (0, pad_seq - dseq), (0, 0))).reshape(dbsz * pad_seq, d)

    outs = ([], [], [], [], [], [])
    for l in range(depth):
        lam_init = 0.8 - 0.6 * math.exp(-0.3 * l)
        w = _prep_weights(g_mix_pre[l], w_in[l], w_gk[l], b_gk[l], g_attn_sub[l], g_gla_sub[l], w_out[l],
                          g_mix_post[l], g_ffn_pre[l], w_gate[l], w_up[l], w_down[l], g_ffn_post[l])
        lams = [v[l].reshape(1, A_DQK).astype(F32) for v in (lambda_q1, lambda_k1, lambda_q2, lambda_k2)]
        proj = functools.partial(_in_proj, g=w["g_mix_pre"], wq_t=w["wq_t"], wv_t=w["wv_t"], wg=w["wg"],
                                 wgk=w["wgk"], bgk=w["bgk"], wm=w["wm"])

        tm = _largest_divisor(seq, (512, 256, 128))
        z, qt, vt, kb, la = proj(x_p, tm=tm)
        blk = _largest_divisor(seq, (512, 256, 128))
        oa = _attn_prompt(qt, kb, vt, *lams, w["g_attn_col"], blk=blk, lam_init=lam_init)
        chunk = _largest_divisor(seq, (128, 64, 32, 16, 8))
        tc = _largest_divisor(seq, (1024, 512, 256, 128, 64, 32, 16, 8))
        ob, s_p = _gla_prompt(z, la, d=d, tc=tc, chunk=chunk, sub=min(16, chunk))
        x_p = _trunk_tail(x_p, oa, ob, z, w)
        outs[0].append(z[:, 0:d].reshape(bsz, seq, heads, A_DV))
        outs[1].append(z[:, d:2 * d].reshape(bsz, seq, heads, A_DV))
        outs[2].append(s_p.reshape(bsz, B_HEADS, dk, dv))

        ts = dbsz * pad_seq
        zs, qts, _, _, las = proj(x_s, tm=_largest_divisor(ts, (256, 128)))
        zs3 = zs.reshape(dbsz, pad_seq, zs.shape[1])
        q_s = qts.T.reshape(dbsz, pad_seq, d)
        n_group = _largest_divisor(n_pages, (8, 4, 2, 1))
        oas = _attn_decode(page_table, q_s, zs3, cache_k[l].reshape(n_pool, page * heads, A_DV),
                           cache_v[l].reshape(n_pool, page * heads, A_DV), *lams, w["g_attn_row"],
                           n_group=n_group, lam_init=lam_init)
        obs, s_s = _gla_sample(zs3, las.reshape(dbsz, pad_seq, las.shape[1]), state_gla[l], d=d, n_valid=dseq)
        x_s = _trunk_tail(x_s, oas.reshape(ts, d), obs.reshape(ts, d), zs, w)
        outs[3].append(zs3[:, :dseq, 0:d].reshape(dbsz, dseq, heads, A_DV))
        outs[4].append(zs3[:, :dseq, d:2 * d].reshape(dbsz, dseq, heads, A_DV))
        outs[5].append(s_s)

    y_p = x_p.reshape(bsz, seq, d)
    y_s = x_s.reshape(dbsz, pad_seq, d)[:, :dseq]
    k_p, v_p, g_p, k_s, v_s, g_s = (jnp.stack(o) for o in outs)
    return (y_p, y_s, k_p, v_p, g_p, k_s, v_s, g_s)
```

```python
import functools
import math

import jax
import jax.numpy as jnp
from jax import lax
from jax.experimental import pallas as pl
from jax.experimental.pallas import tpu as pltpu

A_DQK = 64
A_DV = 2 * A_DQK
B_HEADS = 4
GATE_TAU = 16.0
EPS = 1e-6

LANES = 128
SUBLANES = 8
VMEM_LIMIT_BYTES = 56 * 1024 * 1024

NEG_BIG = -1e30
LOG2_E = math.log2(math.e)
ATTN_SCORE_BUFFERS = 3
F32 = jnp.float32
BF16 = jnp.bfloat16


def _params(*semantics):
    return pltpu.CompilerParams(dimension_semantics=semantics, vmem_limit_bytes=VMEM_LIMIT_BYTES)


def _rms(x, g):
    return x * lax.rsqrt(jnp.mean(x * x, axis=-1, keepdims=True) + EPS) * g


def _lambda(lq1, lk1, lq2, lk2, lam_init):
    s1 = jnp.sum(lq1 * lk1, axis=-1, keepdims=True)
    s2 = jnp.sum(lq2 * lk2, axis=-1, keepdims=True)
    return jnp.exp(s1) - jnp.exp(s2) + lam_init


def _attn_proj_kernel(x_ref, g_ref, wq_t_ref, wv_t_ref, wk_ref, wv_ref,
                      qt_ref, vt_ref, kb_ref, ko_ref, vo_ref, *, q_scale, heads):
    tm = x_ref.shape[0]
    h = _rms(x_ref[...], g_ref[...]).astype(BF16)
    nt = (((1,), (1,)), ((), ()))
    qt = lax.dot_general(wq_t_ref[...], h, nt, preferred_element_type=F32)
    qt_ref[...] = (qt * q_scale).astype(BF16)
    vt = lax.dot_general(wv_t_ref[...], h, nt, preferred_element_type=F32)
    vt_ref[...] = vt.astype(BF16)
    k = jnp.dot(h, wk_ref[...], preferred_element_type=F32)
    kb_ref[...] = k.astype(BF16)
    v = jnp.dot(h, wv_ref[...], preferred_element_type=F32)
    for hd in range(heads):
        ko_ref[pl.ds(hd, tm, stride=heads), :] = k[:, hd * A_DV:(hd + 1) * A_DV]
        vo_ref[pl.ds(hd, tm, stride=heads), :] = v[:, hd * A_DV:(hd + 1) * A_DV]


def _attn_proj(x, g, wq_t, wv_t, wk, wv, *, tm):
    t, d = x.shape
    heads = d // A_DV
    const = lambda i: (0, 0)
    return pl.pallas_call(
        functools.partial(_attn_proj_kernel, q_scale=A_DQK ** -0.5 * LOG2_E, heads=heads),
        grid=(t // tm,),
        in_specs=[
            pl.BlockSpec((tm, d), lambda i: (i, 0)),
            pl.BlockSpec((1, d), const),
            pl.BlockSpec((d, d), const),
            pl.BlockSpec((d, d), const),
            pl.BlockSpec((d, d), const),
            pl.BlockSpec((d, d), const),
        ],
        out_specs=[
            pl.BlockSpec((d, tm), lambda i: (0, i)),
            pl.BlockSpec((d, tm), lambda i: (0, i)),
            pl.BlockSpec((tm, d), lambda i: (i, 0)),
            pl.BlockSpec((tm * heads, A_DV), lambda i: (i, 0)),
            pl.BlockSpec((tm * heads, A_DV), lambda i: (i, 0)),
        ],
        out_shape=[
            jax.ShapeDtypeStruct((d, t), BF16),
            jax.ShapeDtypeStruct((d, t), BF16),
            jax.ShapeDtypeStruct((t, d), BF16),
            jax.ShapeDtypeStruct((t * heads, A_DV), F32),
            jax.ShapeDtypeStruct((t * heads, A_DV), F32),
        ],
        compiler_params=_params("parallel"),
        name="attn_proj",
    )(x, g, wq_t, wv_t, wk, wv)


def _mix_proj_kernel(x_ref, g_ref, wg_ref, wgk_ref, bgk_ref, wm_ref, z_ref, la_ref, h_ref):
    @pl.when(pl.program_id(1) == 0)
    def _():
        h = _rms(x_ref[...], g_ref[...]).astype(BF16)
        h_ref[...] = h
        g_lr = jnp.dot(h, wg_ref[...], preferred_element_type=F32)
        gk = jnp.dot(g_lr.astype(BF16), wgk_ref[...], preferred_element_type=F32) + bgk_ref[...]
        la_ref[...] = jax.nn.log_sigmoid(gk) / GATE_TAU

    z_ref[...] = jnp.dot(h_ref[...], wm_ref[...], preferred_element_type=F32)


def _mix_proj(x, g, wg, wgk, bgk, wm, *, tm):
    t, d = x.shape
    n_tiles = wm.shape[1] // d
    wk = wgk.shape[1]
    const = lambda i, j: (0, 0)
    return pl.pallas_call(
        _mix_proj_kernel,
        grid=(t // tm, n_tiles),
        in_specs=[
            pl.BlockSpec((tm, d), lambda i, j: (i, 0)),
            pl.BlockSpec((1, d), const),
            pl.BlockSpec((d, LANES), const),
            pl.BlockSpec((LANES, wk), const),
            pl.BlockSpec((1, wk), const),
            pl.BlockSpec((d, d), lambda i, j: (0, j)),
        ],
        out_specs=[
            pl.BlockSpec((tm, d), lambda i, j: (i, j)),
            pl.BlockSpec((tm, wk), lambda i, j: (i, 0)),
        ],
        out_shape=[
            jax.ShapeDtypeStruct((t, n_tiles * d), F32),
            jax.ShapeDtypeStruct((t, wk), F32),
        ],
        scratch_shapes=[pltpu.VMEM((tm, d), BF16)],
        compiler_params=_params("parallel", "arbitrary"),
        name="mix_proj",
    )(x, g, wg, wgk, bgk, wm)


def _attn_prompt_kernel(qt_ref, k_ref, vt_ref, lq1_ref, lk1_ref, lq2_ref, lk2_ref, gcol_ref,
                        o_ref, qs_ref, s_refs, m_ref, l_ref, acc_ref, *, blk, lam_init):
    n_buf = len(s_refs)
    i = pl.program_id(1)
    q = qt_ref[...]
    row = lax.broadcasted_iota(jnp.int32, q.shape, 0)
    zero = jnp.zeros_like(q)
    qs_ref[0] = jnp.where(row < A_DQK, q, zero)
    qs_ref[1] = jnp.where(row >= A_DQK, q, zero)

    m_ref[...] = jnp.full(m_ref.shape, NEG_BIG, F32)
    l_ref[...] = jnp.zeros(l_ref.shape, F32)
    acc_ref[...] = jnp.zeros(acc_ref.shape, F32)

    def scores(kb, slot):
        start = pl.multiple_of(kb * blk, blk)
        k = k_ref[pl.ds(start, blk), :]
        for mi in range(2):
            s_refs[slot][mi] = jnp.dot(k, qs_ref[mi], preferred_element_type=F32)

    def absorb(kb, slot, masked):
        start = pl.multiple_of(kb * blk, blk)
        vt = vt_ref[:, pl.ds(start, blk)]
        for mi in range(2):
            s = s_refs[slot][mi]
            if masked:
                kpos = lax.broadcasted_iota(jnp.int32, s.shape, 0)
                qpos = lax.broadcasted_iota(jnp.int32, s.shape, 1)
                s = jnp.where(kpos <= qpos, s, NEG_BIG)
            m_old = m_ref[mi]
            m_new = jnp.maximum(m_old, jnp.max(s, axis=0, keepdims=True))
            alpha = jnp.exp2(m_old - m_new)
            p = jnp.exp2(s - m_new)
            l_ref[mi] = alpha * l_ref[mi] + jnp.sum(p, axis=0, keepdims=True)
            acc_ref[mi] = alpha * acc_ref[mi] + jnp.dot(vt, p.astype(BF16), preferred_element_type=F32)
            m_ref[mi] = m_new

    ahead = n_buf - 1
    scores(0, 0)
    for j in range(1, ahead):
        pl.when(j <= i)(functools.partial(scores, j, j))

    n_full = jnp.maximum(i - n_buf + 2, 0) // n_buf

    def full_round(r, carry):
        kb = r * n_buf
        for j in range(n_buf):
            scores(kb + j + ahead, (j + ahead) % n_buf)
            absorb(kb + j, j, False)
        return carry

    lax.fori_loop(0, n_full, full_round, 0)

    kb0 = n_full * n_buf
    rest = i - kb0 + 1
    for count in range(1, 2 * n_buf - 1):
        def tail(count=count):
            for j in range(count):
                if ahead <= j + ahead < count:
                    scores(kb0 + j + ahead, (j + ahead) % n_buf)
                absorb(kb0 + j, j % n_buf, j == count - 1)
        pl.when(rest == count)(tail)

    lam = _lambda(lq1_ref[...], lk1_ref[...], lq2_ref[...], lk2_ref[...], lam_init)
    o = acc_ref[0] / l_ref[0] - lam * (acc_ref[1] / l_ref[1])
    ms = jnp.mean(o * o, axis=0, keepdims=True)
    y = o * lax.rsqrt(ms + EPS) * gcol_ref[...] * (1.0 - lam_init)
    o_ref[...] = y.T


def _attn_prompt(qt, kb, vt, lq1, lk1, lq2, lk2, gcol, *, blk, lam_init):
    d, t = qt.shape
    heads = d // A_DV
    vec = pl.BlockSpec((1, A_DQK), lambda h, i: (0, 0))
    return pl.pallas_call(
        functools.partial(_attn_prompt_kernel, blk=blk, lam_init=lam_init),
        grid=(heads, t // blk),
        in_specs=[
            pl.BlockSpec((A_DV, blk), lambda h, i: (h, i)),
            pl.BlockSpec((t, A_DV), lambda h, i: (0, h)),
            pl.BlockSpec((A_DV, t), lambda h, i: (h, 0)),
            vec, vec, vec, vec,
            pl.BlockSpec((A_DV, 1), lambda h, i: (0, 0)),
        ],
        out_specs=pl.BlockSpec((blk, A_DV), lambda h, i: (i, h)),
        out_shape=jax.ShapeDtypeStruct((t, d), F32),
        scratch_shapes=[
            pltpu.VMEM((2, A_DV, blk), BF16),
            tuple(pltpu.VMEM((2, blk, blk), F32) for _ in range(ATTN_SCORE_BUFFERS)),
            pltpu.VMEM((2, 1, blk), F32),
            pltpu.VMEM((2, 1, blk), F32),
            pltpu.VMEM((2, A_DV, blk), F32),
        ],
        compiler_params=_params("parallel", "arbitrary"),
        name="attn_prompt",
    )(qt, kb, vt, lq1, lk1, lq2, lk2, gcol)


def _attn_decode_kernel(pt_ref, q_ref, kn_ref, vn_ref, lq1_ref, lk1_ref, lq2_ref, lk2_ref, g_ref,
                        *rest, n_group, n_tok, lam_init):
    k_refs = rest[:n_group]
    v_refs = rest[n_group:2 * n_group]
    o_ref, qbd_ref, m_ref, l_ref, acc_ref, kpage_ref, vpage_ref, kcat_ref, vcat_ref = rest[2 * n_group:]
    j = pl.program_id(1)
    rows, d = qbd_ref.shape
    page = kpage_ref.shape[0]
    heads = d // A_DV

    def update(k, v, mask):
        nt = (((1,), (1,)), ((), ()))
        s = lax.dot_general(qbd_ref[...], k, nt, preferred_element_type=F32)
        if mask is not None:
            s = jnp.where(mask, s, NEG_BIG)
        m_old = m_ref[...]
        m_new = jnp.maximum(m_old, jnp.max(s, axis=1, keepdims=True))
        alpha = jnp.exp2(m_old - m_new)
        p = jnp.exp2(s - m_new)
        l_ref[...] = alpha * l_ref[...] + jnp.sum(p, axis=1, keepdims=True)
        acc_ref[...] = alpha * acc_ref[...] + jnp.dot(p.astype(BF16), v, preferred_element_type=F32)
        m_ref[...] = m_new

    def gather_heads(ref, n_keys):
        cols = [ref[0, pl.ds(h, n_keys, stride=heads), :] for h in range(heads)]
        return jnp.concatenate(cols, axis=1).astype(BF16)

    @pl.when(j == 0)
    def _():
        q = q_ref[0]
        qrep = jnp.concatenate([q] * (rows // n_tok), axis=0)
        r = lax.broadcasted_iota(jnp.int32, (rows, d), 0)
        c = lax.broadcasted_iota(jnp.int32, (rows, d), 1)
        qbd_ref[...] = jnp.where(r // n_tok == c // A_DQK, qrep, jnp.zeros_like(qrep))
        m_ref[...] = jnp.full(m_ref.shape, NEG_BIG, F32)
        l_ref[...] = jnp.zeros(l_ref.shape, F32)
        acc_ref[...] = jnp.zeros(acc_ref.shape, F32)
        kpage_ref[...] = jnp.zeros(kpage_ref.shape, BF16)
        vpage_ref[...] = jnp.zeros(vpage_ref.shape, BF16)
        kpage_ref[0:n_tok, :] = gather_heads(kn_ref, n_tok)
        vpage_ref[0:n_tok, :] = gather_heads(vn_ref, n_tok)
        tq = lax.broadcasted_iota(jnp.int32, (rows, page), 0) % n_tok
        ts = lax.broadcasted_iota(jnp.int32, (rows, page), 1)
        update(kpage_ref[...], vpage_ref[...], ts <= tq)

    def load_pages(refs, dst_ref):
        for g_idx, ref in enumerate(refs):
            dst_ref[g_idx * page:(g_idx + 1) * page, :] = gather_heads(ref, page)
        return dst_ref[...]

    update(load_pages(k_refs, kcat_ref), load_pages(v_refs, vcat_ref), None)

    @pl.when(j == pl.num_programs(1) - 1)
    def _():
        lam = _lambda(lq1_ref[...], lk1_ref[...], lq2_ref[...], lk2_ref[...], lam_init)
        on = acc_ref[...] / l_ref[...]
        g = g_ref[...]
        for h in range(heads):
            r0 = h * 2 * n_tok
            c0 = h * A_DV
            o = on[r0:r0 + n_tok, c0:c0 + A_DV] - lam * on[r0 + n_tok:r0 + 2 * n_tok, c0:c0 + A_DV]
            o_ref[0, :, c0:c0 + A_DV] = _rms(o, g) * (1.0 - lam_init)


def _attn_decode(page_table, q, k_new, v_new, cache_k, cache_v, lq1, lk1, lq2, lk2, g, *, n_group, lam_init):
    bsz, n_tok, d = q.shape
    n_pages = page_table.shape[1]
    heads = d // A_DV
    page = cache_k.shape[1] // heads
    rows = (d // A_DQK) * n_tok
    vec = pl.BlockSpec((1, A_DQK), lambda b, j, pt: (0, 0))

    def page_spec(g_idx):
        return pl.BlockSpec((1, page * heads, A_DV), lambda b, j, pt: (pt[b, j * n_group + g_idx], 0, 0))

    grid_spec = pltpu.PrefetchScalarGridSpec(
        num_scalar_prefetch=1,
        grid=(bsz, n_pages // n_group),
        in_specs=[
            pl.BlockSpec((1, n_tok, d), lambda b, j, pt: (b, 0, 0)),
            pl.BlockSpec((1, n_tok * heads, A_DV), lambda b, j, pt: (b, 0, 0)),
            pl.BlockSpec((1, n_tok * heads, A_DV), lambda b, j, pt: (b, 0, 0)),
            vec, vec, vec, vec,
            pl.BlockSpec((1, A_DV), lambda b, j, pt: (0, 0)),
        ] + [page_spec(g_idx) for g_idx in range(n_group)] * 2,
        out_specs=pl.BlockSpec((1, n_tok, d), lambda b, j, pt: (b, 0, 0)),
        scratch_shapes=[
            pltpu.VMEM((rows, d), BF16),
            pltpu.VMEM((rows, 1), F32),
            pltpu.VMEM((rows, 1), F32),
            pltpu.VMEM((rows, d), F32),
            pltpu.VMEM((page, d), BF16),
            pltpu.VMEM((page, d), BF16),
            pltpu.VMEM((n_group * page, d), BF16),
            pltpu.VMEM((n_group * page, d), BF16),
        ],
    )
    return pl.pallas_call(
        functools.partial(_attn_decode_kernel, n_group=n_group, n_tok=n_tok, lam_init=lam_init),
        grid_spec=grid_spec,
        out_shape=jax.ShapeDtypeStruct((bsz, n_tok, d), F32),
        compiler_params=_params("parallel", "arbitrary"),
        name="attn_decode",
    )(page_table, q, k_new, v_new, lq1, lk1, lq2, lk2, g, *([cache_k] * n_group), *([cache_v] * n_group))


def _cumsum_rows(la):
    c = la.shape[0]
    t_i = lax.broadcasted_iota(jnp.int32, (c, c), 0)
    s_i = lax.broadcasted_iota(jnp.int32, (c, c), 1)
    tri = (s_i <= t_i).astype(F32)
    return jnp.dot(tri, la, precision=lax.Precision.HIGHEST, preferred_element_type=F32)


def _gla_chunk(q, k, v, b, st, *, sub, scale):
    c, dk = q.shape
    q = q * scale
    nt = (((1,), (1,)), ((), ()))

    o = lax.dot_general((q * jnp.exp(b)).astype(BF16), st.astype(BF16), nt, preferred_element_type=F32)

    lane = lax.broadcasted_iota(jnp.int32, (sub, c), 1)
    trow = lax.broadcasted_iota(jnp.int32, (sub, c), 0)
    a_rows = []
    for blk in range(c // sub):
        r0 = blk * sub
        qi = q[r0:r0 + sub]
        bi = b[r0:r0 + sub]
        a_blk = jnp.zeros((sub, c), F32)
        if blk > 0:
            ref = b[r0:r0 + 1]
            qs = qi * jnp.exp(bi - ref)
            ks = k * jnp.exp(jnp.minimum(ref - b, 0.0))
            a_off = lax.dot_general(qs.astype(BF16), ks.astype(BF16), nt, preferred_element_type=F32)
            a_blk = jnp.where(lane < r0, a_off, a_blk)
        for s in range(sub):
            ks_row = k[r0 + s:r0 + s + 1]
            bs_row = b[r0 + s:r0 + s + 1]
            x = qi * ks_row * jnp.exp(jnp.minimum(bi - bs_row, 0.0))
            col = jnp.sum(x, axis=1, keepdims=True)
            a_blk = jnp.where((lane == r0 + s) & (trow >= s), col, a_blk)
        a_rows.append(a_blk)
    a = a_rows[0] if len(a_rows) == 1 else jnp.concatenate(a_rows, axis=0)
    o = o + jnp.dot(a.astype(BF16), v.astype(BF16), preferred_element_type=F32)

    bl = b[c - 1:c]
    kd = k * jnp.exp(bl - b)
    tn = (((0,), (0,)), ((), ()))
    st_new = st * jnp.exp(bl) + lax.dot_general(v.astype(BF16), kd.astype(BF16), tn, preferred_element_type=F32)
    return o, st_new


def _gla_prompt_kernel(qk_ref, v_ref, la_ref, o_ref, s_ref, st_ref, *, chunk, sub, scale):
    t = pl.program_id(0)
    heads, dv, dk = st_ref.shape

    @pl.when(t == 0)
    def _():
        st_ref[...] = jnp.zeros(st_ref.shape, F32)

    def body(ci, carry):
        r0 = pl.multiple_of(ci * chunk, chunk)
        rows = pl.ds(r0, chunk)
        b_all = _cumsum_rows(la_ref[rows, :])
        for h in range(heads):
            o, st_new = _gla_chunk(qk_ref[rows, h * dk:(h + 1) * dk],
                                   qk_ref[rows, (heads + h) * dk:(heads + h + 1) * dk],
                                   v_ref[rows, h * dv:(h + 1) * dv],
                                   b_all[:, h * dk:(h + 1) * dk], st_ref[h], sub=sub, scale=scale)
            o_ref[rows, h * dv:(h + 1) * dv] = o
            st_ref[h] = st_new
        return carry

    lax.fori_loop(0, qk_ref.shape[0] // chunk, body, 0)

    @pl.when(t == pl.num_programs(0) - 1)
    def _():
        for h in range(heads):
            s_ref[h] = st_ref[h].T


def _gla_prompt(z, la, *, d, tc, chunk, sub):
    t = z.shape[0]
    dk = d // (2 * B_HEADS)
    dv = d // B_HEADS
    return pl.pallas_call(
        functools.partial(_gla_prompt_kernel, chunk=chunk, sub=sub, scale=dk ** -0.5),
        grid=(t // tc,),
        in_specs=[
            pl.BlockSpec((tc, d), lambda i: (i, 0)),
            pl.BlockSpec((tc, d), lambda i: (i, 1)),
            pl.BlockSpec((tc, B_HEADS * dk), lambda i: (i, 0)),
        ],
        out_specs=[
            pl.BlockSpec((tc, d), lambda i: (i, 0)),
            pl.BlockSpec((B_HEADS, dk, dv), lambda i: (0, 0, 0)),
        ],
        out_shape=[
            jax.ShapeDtypeStruct((t, d), F32),
            jax.ShapeDtypeStruct((B_HEADS, dk, dv), F32),
        ],
        scratch_shapes=[pltpu.VMEM((B_HEADS, dv, dk), F32)],
        compiler_params=_params("arbitrary"),
        name="gla_prompt",
    )(z, z, la)


def _gla_sample_kernel(qk_ref, v_ref, la_ref, s0_ref, o_ref, s_ref, *, n_valid, scale):
    c = qk_ref.shape[1]
    heads, dk, dv = s0_ref.shape[1:]
    valid = lax.broadcasted_iota(jnp.int32, (c, 1), 0) < n_valid
    b_all = _cumsum_rows(jnp.where(valid, la_ref[0], 0.0))
    for h in range(heads):
        k = jnp.where(valid, qk_ref[0, :, (heads + h) * dk:(heads + h + 1) * dk], 0.0)
        o, st_new = _gla_chunk(qk_ref[0, :, h * dk:(h + 1) * dk], k, v_ref[0, :, h * dv:(h + 1) * dv],
                               b_all[:, h * dk:(h + 1) * dk], s0_ref[0, h].T, sub=c, scale=scale)
        o_ref[0, :, h * dv:(h + 1) * dv] = o
        s_ref[0, h] = st_new.T


def _gla_sample(z3, la3, s0, *, d, n_valid):
    bsz, c, _ = z3.shape
    state_spec = pl.BlockSpec((1,) + s0.shape[1:], lambda b: (b, 0, 0, 0))
    return pl.pallas_call(
        functools.partial(_gla_sample_kernel, n_valid=n_valid, scale=s0.shape[2] ** -0.5),
        grid=(bsz,),
        in_specs=[
            pl.BlockSpec((1, c, d), lambda b: (b, 0, 0)),
            pl.BlockSpec((1, c, d), lambda b: (b, 0, 1)),
            pl.BlockSpec((1, c, la3.shape[2]), lambda b: (b, 0, 0)),
            state_spec,
        ],
        out_specs=[pl.BlockSpec((1, c, d), lambda b: (b, 0, 0)), state_spec],
        out_shape=[
            jax.ShapeDtypeStruct((bsz, c, d), F32),
            jax.ShapeDtypeStruct(s0.shape, F32),
        ],
        compiler_params=_params("parallel"),
        name="gla_sample",
    )(z3, z3, la3, s0)


def _mix_out_kernel(x_ref, oa_ref, ob_ref, r_ref, ga_ref, gb_ref, gsub_ref, wo_ref, gpost_ref, y_ref):
    dv = gsub_ref.shape[1]
    ob = ob_ref[...]
    gsub = gsub_ref[...]
    ob_n = jnp.concatenate(
        [_rms(ob[:, h * dv:(h + 1) * dv], gsub) for h in range(ob.shape[1] // dv)], axis=1)
    ob_n = ob_n * jax.nn.silu(r_ref[...])
    merged = jax.nn.sigmoid(ga_ref[...]) * oa_ref[...] + jax.nn.sigmoid(gb_ref[...]) * ob_n
    y = jnp.dot(merged.astype(BF16), wo_ref[...], preferred_element_type=F32)
    y_ref[...] = x_ref[...] + _rms(y, gpost_ref[...])


def _mix_out(x, oa, ob, z, gsub, wo, gpost, *, tm):
    t, d = x.shape
    row = lambda i: (i, 0)
    const = lambda i: (0, 0)
    return pl.pallas_call(
        _mix_out_kernel,
        grid=(t // tm,),
        in_specs=[
            pl.BlockSpec((tm, d), row),
            pl.BlockSpec((tm, d), row),
            pl.BlockSpec((tm, d), row),
            pl.BlockSpec((tm, d), lambda i: (i, 2)),
            pl.BlockSpec((tm, d), lambda i: (i, 3)),
            pl.BlockSpec((tm, d), lambda i: (i, 4)),
            pl.BlockSpec((1, gsub.shape[1]), const),
            pl.BlockSpec((d, d), const),
            pl.BlockSpec((1, d), const),
        ],
        out_specs=pl.BlockSpec((tm, d), row),
        out_shape=jax.ShapeDtypeStruct((t, d), F32),
        compiler_params=_params("parallel"),
        name="mix_out",
    )(x, oa, ob, z, z, z, gsub, wo, gpost)


def _ffn_kernel(x_ref, gpre_ref, wg_ref, wu_ref, wd_ref, gpost_ref, y_ref, h_ref, acc_ref):
    j = pl.program_id(1)

    @pl.when(j == 0)
    def _():
        h_ref[...] = _rms(x_ref[...], gpre_ref[...]).astype(BF16)
        acc_ref[...] = jnp.zeros(acc_ref.shape, F32)

    h = h_ref[...]
    gate = jnp.dot(h, wg_ref[...], preferred_element_type=F32)
    up = jnp.dot(h, wu_ref[...], preferred_element_type=F32)
    act = (jax.nn.silu(gate) * up).astype(BF16)
    acc_ref[...] += jnp.dot(act, wd_ref[...], preferred_element_type=F32)

    @pl.when(j == pl.num_programs(1) - 1)
    def _():
        y_ref[...] = x_ref[...] + _rms(acc_ref[...], gpost_ref[...])


def _ffn(x, gpre, wg, wu, wd, gpost, *, tm, tf):
    t, d = x.shape
    dff = wg.shape[1]
    row = lambda i, j: (i, 0)
    const = lambda i, j: (0, 0)
    return pl.pallas_call(
        _ffn_kernel,
        grid=(t // tm, dff // tf),
        in_specs=[
            pl.BlockSpec((tm, d), row),
            pl.BlockSpec((1, d), const),
            pl.BlockSpec((d, tf), lambda i, j: (0, j)),
            pl.BlockSpec((d, tf), lambda i, j: (0, j)),
            pl.BlockSpec((tf, d), lambda i, j: (j, 0)),
            pl.BlockSpec((1, d), const),
        ],
        out_specs=pl.BlockSpec((tm, d), row),
        out_shape=jax.ShapeDtypeStruct((t, d), F32),
        scratch_shapes=[pltpu.VMEM((tm, d), BF16), pltpu.VMEM((tm, d), F32)],
        compiler_params=_params("parallel", "arbitrary"),
        name="ffn",
    )(x, gpre, wg, wu, wd, gpost)


def _largest_divisor(n, candidates):
    for c in candidates:
        if n % c == 0:
            return c
    raise ValueError(f"no supported tile for extent {n}")


def _prep_weights(g_mix_pre, w_in, w_gk, b_gk, g_attn_sub, g_gla_sub, w_out, g_mix_post,
                  g_ffn_pre, w_gate, w_up, w_down, g_ffn_post):
    d = w_in.shape[0]
    rank = w_gk.shape[0]
    a_w, wk, wv = d, d // 2, d
    off = [0]
    for s in (a_w, a_w, a_w, wk, wk, wv, rank, wv, d, d):
        off.append(off[-1] + s)
    col = lambda a, b: w_in[:, off[a]:off[b]]
    wq_t = col(0, 1).T.astype(BF16)
    wv_t = col(2, 3).T.astype(BF16)
    wm = jnp.concatenate([col(3, 5), col(5, 6), col(7, 8), col(8, 9), col(9, 10)], axis=1).astype(BF16)
    wg = jnp.pad(col(6, 7), ((0, 0), (0, LANES - rank))).astype(BF16)
    wgk = jnp.pad(w_gk, ((0, LANES - rank), (0, 0))).astype(BF16)
    row = lambda v: v.reshape(1, -1).astype(F32)
    return dict(
        g_mix_pre=row(g_mix_pre), wq_t=wq_t, wv_t=wv_t, wk=col(1, 2).astype(BF16), wv=col(2, 3).astype(BF16),
        wm=wm, wg=wg, wgk=wgk, bgk=row(b_gk),
        g_attn_row=row(g_attn_sub), g_attn_col=g_attn_sub.reshape(-1, 1).astype(F32),
        g_gla_sub=row(g_gla_sub), w_out=w_out.astype(BF16), g_mix_post=row(g_mix_post),
        g_ffn_pre=row(g_ffn_pre), w_gate=w_gate.astype(BF16), w_up=w_up.astype(BF16),
        w_down=w_down.astype(BF16), g_ffn_post=row(g_ffn_post))


def _trunk_tail(x2, oa, ob, z, w):
    t = x2.shape[0]
    tm = _largest_divisor(t, (512, 256, 128, 64, 32, 16, 8))
    x1 = _mix_out(x2, oa, ob, z, w["g_gla_sub"], w["w_out"], w["g_mix_post"], tm=tm)
    dff = w["w_gate"].shape[1]
    tf = dff // 2 if (dff // 2) % LANES == 0 else dff
    tmf = _largest_divisor(t, (1024, 512, 256, 128, 64, 32, 16, 8))
    return _ffn(x1, w["g_ffn_pre"], w["w_gate"], w["w_up"], w["w_down"], w["g_ffn_post"], tm=tmf, tf=tf)


def kernel(x_prompt, x_sample, cache_k, cache_v, state_gla, page_table, g_mix_pre, w_in, w_gk, b_gk,
           lambda_q1, lambda_k1, lambda_q2, lambda_k2, g_attn_sub, g_gla_sub, w_out, g_mix_post,
           g_ffn_pre, w_gate, w_up, w_down, g_ffn_post):
    depth = w_in.shape[0]
    bsz, seq, d = x_prompt.shape
    dbsz, dseq, _ = x_sample.shape
    heads = d // A_DV
    page = cache_k.shape[2]
    n_pool = cache_k.shape[1]
    n_pages = page_table.shape[1]
    dk = d // (2 * B_HEADS)
    dv = d // B_HEADS
    assert bsz == 1, "prompt kernels take one sequence"
    pad_seq = -(-dseq // SUBLANES) * SUBLANES

    x_p = x_prompt.reshape(seq, d)
    x_s = jnp.pad(x_sample, ((0, 0), (0, pad_seq - dseq), (0, 0))).reshape(dbsz * pad_seq, d)

    outs = ([], [], [], [], [], [])
    for l in range(depth):
        lam_init = 0.8 - 0.6 * math.exp(-0.3 * l)
        w = _prep_weights(g_mix_pre[l], w_in[l], w_gk[l], b_gk[l], g_attn_sub[l], g_gla_sub[l], w_out[l],
                          g_mix_post[l], g_ffn_pre[l], w_gate[l], w_up[l], w_down[l], g_ffn_post[l])
        lams = [v[l].reshape(1, A_DQK).astype(F32) for v in (lambda_q1, lambda_k1, lambda_q2, lambda_k2)]
        attn_proj = functools.partial(_attn_proj, g=w["g_mix_pre"], wq_t=w["wq_t"], wv_t=w["wv_t"],
                                      wk=w["wk"], wv=w["wv"])
        mix_proj = functools.partial(_mix_proj, g=w["g_mix_pre"], wg=w["wg"], wgk=w["wgk"], bgk=w["bgk"],
                                     wm=w["wm"])

        qt, vt, kb, k_p, v_p = attn_proj(x_p, tm=_largest_divisor(seq, (512, 256, 128)))
        z, la = mix_proj(x_p, tm=_largest_divisor(seq, (1024, 512, 256, 128)))
        blk = _largest_divisor(seq, (512, 256, 128))
        oa = _attn_prompt(qt, kb, vt, *lams, w["g_attn_col"], blk=blk, lam_init=lam_init)
        chunk = _largest_divisor(seq, (128, 64, 32, 16, 8))
        tc = _largest_divisor(seq, (1024, 512, 256, 128, 64, 32, 16, 8))
        ob, s_p = _gla_prompt(z, la, d=d, tc=tc, chunk=chunk, sub=min(16, chunk))
        x_p = _trunk_tail(x_p, oa, ob, z, w)
        outs[0].append(k_p.reshape(bsz, seq, heads, A_DV))
        outs[1].append(v_p.reshape(bsz, seq, heads, A_DV))
        outs[2].append(s_p.reshape(bsz, B_HEADS, dk, dv))

        ts = dbsz * pad_seq
        tms = _largest_divisor(ts, (256, 128))
        qts, _, _, k_s, v_s = attn_proj(x_s, tm=tms)
        zs, las = mix_proj(x_s, tm=tms)
        zs3 = zs.reshape(dbsz, pad_seq, zs.shape[1])
        q_s = qts.T.reshape(dbsz, pad_seq, d)
        k_s = k_s.reshape(dbsz, pad_seq * heads, A_DV)
        v_s = v_s.reshape(dbsz, pad_seq * heads, A_DV)
        n_group = _largest_divisor(n_pages, (16, 8, 4, 2, 1))
        oas = _attn_decode(page_table, q_s, k_s, v_s, cache_k[l].reshape(n_pool, page * heads, A_DV),
                           cache_v[l].reshape(n_pool, page * heads, A_DV), *lams, w["g_attn_row"],
                           n_group=n_group, lam_init=lam_init)
        obs, s_s = _gla_sample(zs3, las.reshape(dbsz, pad_seq, las.shape[1]), state_gla[l], d=d, n_valid=dseq)
        x_s = _trunk_tail(x_s, oas.reshape(ts, d), obs.reshape(ts, d), zs, w)
        outs[3].append(k_s.reshape(dbsz, pad_seq, heads, A_DV)[:, :dseq])
        outs[4].append(v_s.reshape(dbsz, pad_seq, heads, A_DV)[:, :dseq])
        outs[5].append(s_s)

    y_p = x_p.reshape(bsz, seq, d)
    y_s = x_s.reshape(dbsz, pad_seq, d)[:, :dseq]
    k_p, v_p, g_p, k_s, v_s, g_s = (jnp.stack(o) for o in outs)
    return (y_p, y_s, k_p, v_p, g_p, k_s, v_s, g_s)
```

```python
import functools
import math

import jax
import jax.numpy as jnp
from jax import lax
from jax.experimental import pallas as pl
from jax.experimental.pallas import tpu as pltpu

A_DQK = 64
A_DV = 2 * A_DQK
B_HEADS = 4
GATE_TAU = 16.0
EPS = 1e-6

LANES = 128
SUBLANES = 8
BF16_SUBLANES = 16
VMEM_LIMIT_BYTES = 56 * 1024 * 1024

NEG_BIG = -1e30
LOG2_E = math.log2(math.e)
ATTN_SCORE_BUFFERS = 3
ATTN_SCORE_LOOKAHEAD = 1
F32 = jnp.float32
BF16 = jnp.bfloat16


def _params(*semantics):
    return pltpu.CompilerParams(dimension_semantics=semantics, vmem_limit_bytes=VMEM_LIMIT_BYTES)


def _rms(x, g):
    return x * lax.rsqrt(jnp.mean(x * x, axis=-1, keepdims=True) + EPS) * g


def _lambda(lq1, lk1, lq2, lk2, lam_init):
    s1 = jnp.sum(lq1 * lk1, axis=-1, keepdims=True)
    s2 = jnp.sum(lq2 * lk2, axis=-1, keepdims=True)
    return jnp.exp(s1) - jnp.exp(s2) + lam_init


def _attn_proj_kernel(x_ref, g_ref, wq_t_ref, wv_t_ref, wk_ref, wv_ref,
                      qt_ref, vt_ref, kb_ref, ko_ref, vo_ref, *, q_scale, heads):
    tm = x_ref.shape[0]
    h = _rms(x_ref[...], g_ref[...]).astype(BF16)
    nt = (((1,), (1,)), ((), ()))
    qt = lax.dot_general(wq_t_ref[...], h, nt, preferred_element_type=F32)
    qt_ref[...] = (qt * q_scale).astype(BF16)
    vt = lax.dot_general(wv_t_ref[...], h, nt, preferred_element_type=F32)
    vt_ref[...] = vt.astype(BF16)
    k = jnp.dot(h, wk_ref[...], preferred_element_type=F32)
    kb_ref[...] = k.astype(BF16)
    v = jnp.dot(h, wv_ref[...], preferred_element_type=F32)
    for hd in range(heads):
        ko_ref[pl.ds(hd, tm, stride=heads), :] = k[:, hd * A_DV:(hd + 1) * A_DV]
        vo_ref[pl.ds(hd, tm, stride=heads), :] = v[:, hd * A_DV:(hd + 1) * A_DV]


def _attn_proj(x, g, wq_t, wv_t, wk, wv, *, tm):
    t, d = x.shape
    heads = d // A_DV
    const = lambda i: (0, 0)
    return pl.pallas_call(
        functools.partial(_attn_proj_kernel, q_scale=A_DQK ** -0.5 * LOG2_E, heads=heads),
        grid=(t // tm,),
        in_specs=[
            pl.BlockSpec((tm, d), lambda i: (i, 0)),
            pl.BlockSpec((1, d), const),
            pl.BlockSpec((d, d), const),
            pl.BlockSpec((d, d), const),
            pl.BlockSpec((d, d), const),
            pl.BlockSpec((d, d), const),
        ],
        out_specs=[
            pl.BlockSpec((d, tm), lambda i: (0, i)),
            pl.BlockSpec((d, tm), lambda i: (0, i)),
            pl.BlockSpec((tm, d), lambda i: (i, 0)),
            pl.BlockSpec((tm * heads, A_DV), lambda i: (i, 0)),
            pl.BlockSpec((tm * heads, A_DV), lambda i: (i, 0)),
        ],
        out_shape=[
            jax.ShapeDtypeStruct((d, t), BF16),
            jax.ShapeDtypeStruct((d, t), BF16),
            jax.ShapeDtypeStruct((t, d), BF16),
            jax.ShapeDtypeStruct((t * heads, A_DV), F32),
            jax.ShapeDtypeStruct((t * heads, A_DV), F32),
        ],
        compiler_params=_params("parallel"),
        name="attn_proj",
    )(x, g, wq_t, wv_t, wk, wv)


def _mix_proj_kernel(x_ref, g_ref, wg_ref, wgk_ref, bgk_ref, wm_ref, z_ref, la_ref, h_ref):
    @pl.when(pl.program_id(1) == 0)
    def _():
        h = _rms(x_ref[...], g_ref[...]).astype(BF16)
        h_ref[...] = h
        g_lr = jnp.dot(h, wg_ref[...], preferred_element_type=F32)
        gk = jnp.dot(g_lr.astype(BF16), wgk_ref[...], preferred_element_type=F32) + bgk_ref[...]
        la_ref[...] = jax.nn.log_sigmoid(gk) / GATE_TAU

    z_ref[...] = jnp.dot(h_ref[...], wm_ref[...], preferred_element_type=F32)


def _mix_proj(x, g, wg, wgk, bgk, wm, *, tm):
    t, d = x.shape
    n_tiles = wm.shape[1] // d
    wk = wgk.shape[1]
    const = lambda i, j: (0, 0)
    return pl.pallas_call(
        _mix_proj_kernel,
        grid=(t // tm, n_tiles),
        in_specs=[
            pl.BlockSpec((tm, d), lambda i, j: (i, 0)),
            pl.BlockSpec((1, d), const),
            pl.BlockSpec((d, LANES), const),
            pl.BlockSpec((LANES, wk), const),
            pl.BlockSpec((1, wk), const),
            pl.BlockSpec((d, d), lambda i, j: (0, j)),
        ],
        out_specs=[
            pl.BlockSpec((tm, d), lambda i, j: (i, j)),
            pl.BlockSpec((tm, wk), lambda i, j: (i, 0)),
        ],
        out_shape=[
            jax.ShapeDtypeStruct((t, n_tiles * d), F32),
            jax.ShapeDtypeStruct((t, wk), F32),
        ],
        scratch_shapes=[pltpu.VMEM((tm, d), BF16)],
        compiler_params=_params("parallel", "arbitrary"),
        name="mix_proj",
    )(x, g, wg, wgk, bgk, wm)


def _attn_prompt_kernel(qt_ref, k_ref, vt_ref, lq1_ref, lk1_ref, lq2_ref, lk2_ref, gcol_ref,
                        o_ref, qs_ref, s_refs, m_ref, l_ref, acc_ref, *, bq, bk, lam_init):
    n_buf = len(s_refs)
    ratio = bq // bk
    i = pl.program_id(1)
    q = qt_ref[...]
    row = lax.broadcasted_iota(jnp.int32, q.shape, 0)
    zero = jnp.zeros_like(q)
    qs_ref[0] = jnp.where(row < A_DQK, q, zero)
    qs_ref[1] = jnp.where(row >= A_DQK, q, zero)

    m_ref[...] = jnp.full(m_ref.shape, NEG_BIG, F32)
    l_ref[...] = jnp.zeros(l_ref.shape, F32)
    acc_ref[...] = jnp.zeros(acc_ref.shape, F32)

    def scores(kb, slot):
        start = pl.multiple_of(kb * bk, bk)
        k = k_ref[pl.ds(start, bk), :]
        for mi in range(2):
            s_refs[slot][mi] = jnp.dot(k, qs_ref[mi], preferred_element_type=F32)

    def absorb(kb, slot, diag):
        start = pl.multiple_of(kb * bk, bk)
        vt = vt_ref[:, pl.ds(start, bk)]
        vt1 = jnp.concatenate([vt, jnp.ones((BF16_SUBLANES, bk), BF16)], axis=0)
        for mi in range(2):
            s = s_refs[slot][mi]
            if diag is not None:
                kpos = lax.broadcasted_iota(jnp.int32, s.shape, 0) + diag * bk
                qpos = lax.broadcasted_iota(jnp.int32, s.shape, 1)
                s = jnp.where(kpos <= qpos, s, NEG_BIG)
            m_old = m_ref[mi]
            m_new = jnp.maximum(m_old, jnp.max(s, axis=0, keepdims=True))
            alpha = jnp.exp2(m_old - m_new)
            p = jnp.exp2(s - m_new).astype(BF16)
            upd = jnp.dot(vt1, p, preferred_element_type=F32)
            l_ref[mi] = alpha * l_ref[mi] + upd[A_DV:A_DV + 1]
            acc_ref[mi] = alpha * acc_ref[mi] + upd[:A_DV]
            m_ref[mi] = m_new

    ahead = ATTN_SCORE_LOOKAHEAD
    assert ahead <= ratio and ahead < n_buf
    for j in range(ahead):
        scores(j, j)

    n_plain = i * ratio
    n_full = n_plain // n_buf

    def full_round(r, carry):
        kb = r * n_buf
        for j in range(n_buf):
            scores(kb + j + ahead, (j + ahead) % n_buf)
            absorb(kb + j, j, None)
        return carry

    lax.fori_loop(0, n_full, full_round, 0)

    kb0 = n_full * n_buf
    rest = n_plain + ratio - kb0
    for count in range(ratio, ratio + n_buf):
        def tail(count=count):
            for j in range(count):
                if j + ahead < count:
                    scores(kb0 + j + ahead, (j + ahead) % n_buf)
                absorb(kb0 + j, j % n_buf, None if j < count - ratio else j - (count - ratio))
        pl.when(rest == count)(tail)

    lam = _lambda(lq1_ref[...], lk1_ref[...], lq2_ref[...], lk2_ref[...], lam_init)
    o = acc_ref[0] / l_ref[0] - lam * (acc_ref[1] / l_ref[1])
    ms = jnp.mean(o * o, axis=0, keepdims=True)
    y = o * lax.rsqrt(ms + EPS) * gcol_ref[...] * (1.0 - lam_init)
    o_ref[...] = y.T


def _attn_prompt(qt, kb, vt, lq1, lk1, lq2, lk2, gcol, *, bq, bk, lam_init):
    d, t = qt.shape
    heads = d // A_DV
    vec = pl.BlockSpec((1, A_DQK), lambda h, i: (0, 0))
    return pl.pallas_call(
        functools.partial(_attn_prompt_kernel, bq=bq, bk=bk, lam_init=lam_init),
        grid=(heads, t // bq),
        in_specs=[
            pl.BlockSpec((A_DV, bq), lambda h, i: (h, i)),
            pl.BlockSpec((t, A_DV), lambda h, i: (0, h)),
            pl.BlockSpec((A_DV, t), lambda h, i: (h, 0)),
            vec, vec, vec, vec,
            pl.BlockSpec((A_DV, 1), lambda h, i: (0, 0)),
        ],
        out_specs=pl.BlockSpec((bq, A_DV), lambda h, i: (i, h)),
        out_shape=jax.ShapeDtypeStruct((t, d), F32),
        scratch_shapes=[
            pltpu.VMEM((2, A_DV, bq), BF16),
            tuple(pltpu.VMEM((2, bk, bq), F32) for _ in range(ATTN_SCORE_BUFFERS)),
            pltpu.VMEM((2, 1, bq), F32),
            pltpu.VMEM((2, 1, bq), F32),
            pltpu.VMEM((2, A_DV, bq), F32),
        ],
        compiler_params=_params("parallel", "arbitrary"),
        name="attn_prompt",
    )(qt, kb, vt, lq1, lk1, lq2, lk2, gcol)


def _attn_decode_kernel(pt_ref, q_ref, kn_ref, vn_ref, lq1_ref, lk1_ref, lq2_ref, lk2_ref, g_ref,
                        *rest, n_group, n_tok, lam_init):
    k_refs = rest[:n_group]
    v_refs = rest[n_group:2 * n_group]
    o_ref, qbd_ref, m_ref, l_ref, acc_ref, kpage_ref, vpage_ref, kcat_ref, vcat_ref = rest[2 * n_group:]
    j = pl.program_id(1)
    rows, d = qbd_ref.shape
    page = kpage_ref.shape[0]
    heads = d // A_DV

    def update(k, v, mask):
        nt = (((1,), (1,)), ((), ()))
        s = lax.dot_general(qbd_ref[...], k, nt, preferred_element_type=F32)
        if mask is not None:
            s = jnp.where(mask, s, NEG_BIG)
        m_old = m_ref[...]
        m_new = jnp.maximum(m_old, jnp.max(s, axis=1, keepdims=True))
        alpha = jnp.exp2(m_old - m_new)
        p = jnp.exp2(s - m_new)
        l_ref[...] = alpha * l_ref[...] + jnp.sum(p, axis=1, keepdims=True)
        acc_ref[...] = alpha * acc_ref[...] + jnp.dot(p.astype(BF16), v, preferred_element_type=F32)
        m_ref[...] = m_new

    def gather_heads(ref, n_keys):
        cols = [ref[0, pl.ds(h, n_keys, stride=heads), :] for h in range(heads)]
        return jnp.concatenate(cols, axis=1).astype(BF16)

    @pl.when(j == 0)
    def _():
        q = q_ref[0]
        qrep = jnp.concatenate([q] * (rows // n_tok), axis=0)
        r = lax.broadcasted_iota(jnp.int32, (rows, d), 0)
        c = lax.broadcasted_iota(jnp.int32, (rows, d), 1)
        qbd_ref[...] = jnp.where(r // n_tok == c // A_DQK, qrep, jnp.zeros_like(qrep))
        m_ref[...] = jnp.full(m_ref.shape, NEG_BIG, F32)
        l_ref[...] = jnp.zeros(l_ref.shape, F32)
        acc_ref[...] = jnp.zeros(acc_ref.shape, F32)
        kpage_ref[...] = jnp.zeros(kpage_ref.shape, BF16)
        vpage_ref[...] = jnp.zeros(vpage_ref.shape, BF16)
        kpage_ref[0:n_tok, :] = gather_heads(kn_ref, n_tok)
        vpage_ref[0:n_tok, :] = gather_heads(vn_ref, n_tok)
        tq = lax.broadcasted_iota(jnp.int32, (rows, page), 0) % n_tok
        ts = lax.broadcasted_iota(jnp.int32, (rows, page), 1)
        update(kpage_ref[...], vpage_ref[...], ts <= tq)

    def load_pages(refs, dst_ref):
        for g_idx, ref in enumerate(refs):
            dst_ref[g_idx * page:(g_idx + 1) * page, :] = gather_heads(ref, page)
        return dst_ref[...]

    update(load_pages(k_refs, kcat_ref), load_pages(v_refs, vcat_ref), None)

    @pl.when(j == pl.num_programs(1) - 1)
    def _():
        lam = _lambda(lq1_ref[...], lk1_ref[...], lq2_ref[...], lk2_ref[...], lam_init)
        on = acc_ref[...] / l_ref[...]
        g = g_ref[...]
        for h in range(heads):
            r0 = h * 2 * n_tok
            c0 = h * A_DV
            o = on[r0:r0 + n_tok, c0:c0 + A_DV] - lam * on[r0 + n_tok:r0 + 2 * n_tok, c0:c0 + A_DV]
            o_ref[0, :, c0:c0 + A_DV] = _rms(o, g) * (1.0 - lam_init)


def _attn_decode(page_table, q, k_new, v_new, cache_k, cache_v, lq1, lk1, lq2, lk2, g, *, n_group, lam_init):
    bsz, n_tok, d = q.shape
    n_pages = page_table.shape[1]
    heads = d // A_DV
    page = cache_k.shape[1] // heads
    rows = (d // A_DQK) * n_tok
    vec = pl.BlockSpec((1, A_DQK), lambda b, j, pt: (0, 0))

    def page_spec(g_idx):
        return pl.BlockSpec((1, page * heads, A_DV), lambda b, j, pt: (pt[b, j * n_group + g_idx], 0, 0))

    grid_spec = pltpu.PrefetchScalarGridSpec(
        num_scalar_prefetch=1,
        grid=(bsz, n_pages // n_group),
        in_specs=[
            pl.BlockSpec((1, n_tok, d), lambda b, j, pt: (b, 0, 0)),
            pl.BlockSpec((1, n_tok * heads, A_DV), lambda b, j, pt: (b, 0, 0)),
            pl.BlockSpec((1, n_tok * heads, A_DV), lambda b, j, pt: (b, 0, 0)),
            vec, vec, vec, vec,
            pl.BlockSpec((1, A_DV), lambda b, j, pt: (0, 0)),
        ] + [page_spec(g_idx) for g_idx in range(n_group)] * 2,
        out_specs=pl.BlockSpec((1, n_tok, d), lambda b, j, pt: (b, 0, 0)),
        scratch_shapes=[
            pltpu.VMEM((rows, d), BF16),
            pltpu.VMEM((rows, 1), F32),
            pltpu.VMEM((rows, 1), F32),
            pltpu.VMEM((rows, d), F32),
            pltpu.VMEM((page, d), BF16),
            pltpu.VMEM((page, d), BF16),
            pltpu.VMEM((n_group * page, d), BF16),
            pltpu.VMEM((n_group * page, d), BF16),
        ],
    )
    return pl.pallas_call(
        functools.partial(_attn_decode_kernel, n_group=n_group, n_tok=n_tok, lam_init=lam_init),
        grid_spec=grid_spec,
        out_shape=jax.ShapeDtypeStruct((bsz, n_tok, d), F32),
        compiler_params=_params("parallel", "arbitrary"),
        name="attn_decode",
    )(page_table, q, k_new, v_new, lq1, lk1, lq2, lk2, g, *([cache_k] * n_group), *([cache_v] * n_group))


def _cumsum_rows(la):
    c = la.shape[0]
    t_i = lax.broadcasted_iota(jnp.int32, (c, c), 0)
    s_i = lax.broadcasted_iota(jnp.int32, (c, c), 1)
    tri = (s_i <= t_i).astype(F32)
    return jnp.dot(tri, la, precision=lax.Precision.HIGHEST, preferred_element_type=F32)


def _gla_chunk(q, k, v, b, st, *, sub, scale):
    c, dk = q.shape
    q = q * scale
    b = b * LOG2_E
    nt = (((1,), (1,)), ((), ()))

    o = lax.dot_general((q * jnp.exp2(b)).astype(BF16), st.astype(BF16), nt, preferred_element_type=F32)

    lane = lax.broadcasted_iota(jnp.int32, (sub, c), 1)
    trow = lax.broadcasted_iota(jnp.int32, (sub, c), 0)
    a_rows = []
    for blk in range(c // sub):
        r0 = blk * sub
        qi = q[r0:r0 + sub]
        bi = b[r0:r0 + sub]
        a_blk = jnp.zeros((sub, c), F32)
        if blk > 0:
            ref = b[r0:r0 + 1]
            qs = qi * jnp.exp2(bi - ref)
            ks = k * jnp.exp2(jnp.minimum(ref - b, 0.0))
            a_off = lax.dot_general(qs.astype(BF16), ks.astype(BF16), nt, preferred_element_type=F32)
            a_blk = jnp.where(lane < r0, a_off, a_blk)
        for s in range(sub):
            ks_row = k[r0 + s:r0 + s + 1]
            bs_row = b[r0 + s:r0 + s + 1]
            x = qi * ks_row * jnp.exp2(bi - bs_row)
            col = jnp.sum(x, axis=1, keepdims=True)
            a_blk = jnp.where((lane == r0 + s) & (trow >= s), col, a_blk)
        a_rows.append(a_blk)
    a = a_rows[0] if len(a_rows) == 1 else jnp.concatenate(a_rows, axis=0)
    o = o + jnp.dot(a.astype(BF16), v.astype(BF16), preferred_element_type=F32)

    bl = b[c - 1:c]
    kd = k * jnp.exp2(bl - b)
    tn = (((0,), (0,)), ((), ()))
    st_new = st * jnp.exp2(bl) + lax.dot_general(v.astype(BF16), kd.astype(BF16), tn, preferred_element_type=F32)
    return o, st_new


def _gla_prompt_kernel(qk_ref, v_ref, la_ref, o_ref, s_ref, st_ref, *, chunk, sub, scale):
    t = pl.program_id(0)
    heads, dv, dk = st_ref.shape

    @pl.when(t == 0)
    def _():
        st_ref[...] = jnp.zeros(st_ref.shape, F32)

    def body(ci, carry):
        r0 = pl.multiple_of(ci * chunk, chunk)
        rows = pl.ds(r0, chunk)
        b_all = _cumsum_rows(la_ref[rows, :])
        for h in range(heads):
            o, st_new = _gla_chunk(qk_ref[rows, h * dk:(h + 1) * dk],
                                   qk_ref[rows, (heads + h) * dk:(heads + h + 1) * dk],
                                   v_ref[rows, h * dv:(h + 1) * dv],
                                   b_all[:, h * dk:(h + 1) * dk], st_ref[h], sub=sub, scale=scale)
            o_ref[rows, h * dv:(h + 1) * dv] = o
            st_ref[h] = st_new
        return carry

    lax.fori_loop(0, qk_ref.shape[0] // chunk, body, 0)

    @pl.when(t == pl.num_programs(0) - 1)
    def _():
        for h in range(heads):
            s_ref[h] = st_ref[h].T


def _gla_prompt(z, la, *, d, tc, chunk, sub):
    t = z.shape[0]
    dk = d // (2 * B_HEADS)
    dv = d // B_HEADS
    return pl.pallas_call(
        functools.partial(_gla_prompt_kernel, chunk=chunk, sub=sub, scale=dk ** -0.5),
        grid=(t // tc,),
        in_specs=[
            pl.BlockSpec((tc, d), lambda i: (i, 0)),
            pl.BlockSpec((tc, d), lambda i: (i, 1)),
            pl.BlockSpec((tc, B_HEADS * dk), lambda i: (i, 0)),
        ],
        out_specs=[
            pl.BlockSpec((tc, d), lambda i: (i, 0)),
            pl.BlockSpec((B_HEADS, dk, dv), lambda i: (0, 0, 0)),
        ],
        out_shape=[
            jax.ShapeDtypeStruct((t, d), F32),
            jax.ShapeDtypeStruct((B_HEADS, dk, dv), F32),
        ],
        scratch_shapes=[pltpu.VMEM((B_HEADS, dv, dk), F32)],
        compiler_params=_params("arbitrary"),
        name="gla_prompt",
    )(z, z, la)


def _gla_sample_kernel(qk_ref, v_ref, la_ref, s0_ref, o_ref, s_ref, *, n_valid, scale):
    c = qk_ref.shape[1]
    heads, dk, dv = s0_ref.shape[1:]
    valid = lax.broadcasted_iota(jnp.int32, (c, 1), 0) < n_valid
    b_all = _cumsum_rows(jnp.where(valid, la_ref[0], 0.0))
    for h in range(heads):
        k = jnp.where(valid, qk_ref[0, :, (heads + h) * dk:(heads + h + 1) * dk], 0.0)
        o, st_new = _gla_chunk(qk_ref[0, :, h * dk:(h + 1) * dk], k, v_ref[0, :, h * dv:(h + 1) * dv],
                               b_all[:, h * dk:(h + 1) * dk], s0_ref[0, h].T, sub=c, scale=scale)
        o_ref[0, :, h * dv:(h + 1) * dv] = o
        s_ref[0, h] = st_new.T


def _gla_sample(z3, la3, s0, *, d, n_valid):
    bsz, c, _ = z3.shape
    state_spec = pl.BlockSpec((1,) + s0.shape[1:], lambda b: (b, 0, 0, 0))
    return pl.pallas_call(
        functools.partial(_gla_sample_kernel, n_valid=n_valid, scale=s0.shape[2] ** -0.5),
        grid=(bsz,),
        in_specs=[
            pl.BlockSpec((1, c, d), lambda b: (b, 0, 0)),
            pl.BlockSpec((1, c, d), lambda b: (b, 0, 1)),
            pl.BlockSpec((1, c, la3.shape[2]), lambda b: (b, 0, 0)),
            state_spec,
        ],
        out_specs=[pl.BlockSpec((1, c, d), lambda b: (b, 0, 0)), state_spec],
        out_shape=[
            jax.ShapeDtypeStruct((bsz, c, d), F32),
            jax.ShapeDtypeStruct(s0.shape, F32),
        ],
        compiler_params=_params("parallel"),
        name="gla_sample",
    )(z3, z3, la3, s0)


def _mix_out_kernel(x_ref, oa_ref, ob_ref, r_ref, ga_ref, gb_ref, gsub_ref, wo_ref, gpost_ref, y_ref):
    dv = gsub_ref.shape[1]
    ob = ob_ref[...]
    gsub = gsub_ref[...]
    ob_n = jnp.concatenate(
        [_rms(ob[:, h * dv:(h + 1) * dv], gsub) for h in range(ob.shape[1] // dv)], axis=1)
    ob_n = ob_n * jax.nn.silu(r_ref[...])
    merged = jax.nn.sigmoid(ga_ref[...]) * oa_ref[...] + jax.nn.sigmoid(gb_ref[...]) * ob_n
    y = jnp.dot(merged.astype(BF16), wo_ref[...], preferred_element_type=F32)
    y_ref[...] = x_ref[...] + _rms(y, gpost_ref[...])


def _mix_out(x, oa, ob, z, gsub, wo, gpost, *, tm):
    t, d = x.shape
    row = lambda i: (i, 0)
    const = lambda i: (0, 0)
    return pl.pallas_call(
        _mix_out_kernel,
        grid=(t // tm,),
        in_specs=[
            pl.BlockSpec((tm, d), row),
            pl.BlockSpec((tm, d), row),
            pl.BlockSpec((tm, d), row),
            pl.BlockSpec((tm, d), lambda i: (i, 2)),
            pl.BlockSpec((tm, d), lambda i: (i, 3)),
            pl.BlockSpec((tm, d), lambda i: (i, 4)),
            pl.BlockSpec((1, gsub.shape[1]), const),
            pl.BlockSpec((d, d), const),
            pl.BlockSpec((1, d), const),
        ],
        out_specs=pl.BlockSpec((tm, d), row),
        out_shape=jax.ShapeDtypeStruct((t, d), F32),
        compiler_params=_params("parallel"),
        name="mix_out",
    )(x, oa, ob, z, z, z, gsub, wo, gpost)


def _ffn_kernel(x_ref, gpre_ref, wg_ref, wu_ref, wd_ref, gpost_ref, y_ref, h_ref, acc_ref):
    j = pl.program_id(1)

    @pl.when(j == 0)
    def _():
        h_ref[...] = _rms(x_ref[...], gpre_ref[...]).astype(BF16)
        acc_ref[...] = jnp.zeros(acc_ref.shape, F32)

    h = h_ref[...]
    gate = jnp.dot(h, wg_ref[...], preferred_element_type=F32)
    up = jnp.dot(h, wu_ref[...], preferred_element_type=F32)
    act = (jax.nn.silu(gate) * up).astype(BF16)
    acc_ref[...] += jnp.dot(act, wd_ref[...], preferred_element_type=F32)

    @pl.when(j == pl.num_programs(1) - 1)
    def _():
        y_ref[...] = x_ref[...] + _rms(acc_ref[...], gpost_ref[...])


def _ffn(x, gpre, wg, wu, wd, gpost, *, tm, tf):
    t, d = x.shape
    dff = wg.shape[1]
    row = lambda i, j: (i, 0)
    const = lambda i, j: (0, 0)
    return pl.pallas_call(
        _ffn_kernel,
        grid=(t // tm, dff // tf),
        in_specs=[
            pl.BlockSpec((tm, d), row),
            pl.BlockSpec((1, d), const),
            pl.BlockSpec((d, tf), lambda i, j: (0, j)),
            pl.BlockSpec((d, tf), lambda i, j: (0, j)),
            pl.BlockSpec((tf, d), lambda i, j: (j, 0)),
            pl.BlockSpec((1, d), const),
        ],
        out_specs=pl.BlockSpec((tm, d), row),
        out_shape=jax.ShapeDtypeStruct((t, d), F32),
        scratch_shapes=[pltpu.VMEM((tm, d), BF16), pltpu.VMEM((tm, d), F32)],
        compiler_params=_params("parallel", "arbitrary"),
        name="ffn",
    )(x, gpre, wg, wu, wd, gpost)


def _largest_divisor(n, candidates):
    for c in candidates:
        if n % c == 0:
            return c
    raise ValueError(f"no supported tile for extent {n}")


def _prep_weights(g_mix_pre, w_in, w_gk, b_gk, g_attn_sub, g_gla_sub, w_out, g_mix_post,
                  g_ffn_pre, w_gate, w_up, w_down, g_ffn_post):
    d = w_in.shape[0]
    rank = w_gk.shape[0]
    a_w, wk, wv = d, d // 2, d
    off = [0]
    for s in (a_w, a_w, a_w, wk, wk, wv, rank, wv, d, d):
        off.append(off[-1] + s)
    col = lambda a, b: w_in[:, off[a]:off[b]]
    wq_t = col(0, 1).T.astype(BF16)
    wv_t = col(2, 3).T.astype(BF16)
    wm = jnp.concatenate([col(3, 5), col(5, 6), col(7, 8), col(8, 9), col(9, 10)], axis=1).astype(BF16)
    wg = jnp.pad(col(6, 7), ((0, 0), (0, LANES - rank))).astype(BF16)
    wgk = jnp.pad(w_gk, ((0, LANES - rank), (0, 0))).astype(BF16)
    row = lambda v: v.reshape(1, -1).astype(F32)
    return dict(
        g_mix_pre=row(g_mix_pre), wq_t=wq_t, wv_t=wv_t, wk=col(1, 2).astype(BF16), wv=col(2, 3).astype(BF16),
        wm=wm, wg=wg, wgk=wgk, bgk=row(b_gk),
        g_attn_row=row(g_attn_sub), g_attn_col=g_attn_sub.reshape(-1, 1).astype(F32),
        g_gla_sub=row(g_gla_sub), w_out=w_out.astype(BF16), g_mix_post=row(g_mix_post),
        g_ffn_pre=row(g_ffn_pre), w_gate=w_gate.astype(BF16), w_up=w_up.astype(BF16),
        w_down=w_down.astype(BF16), g_ffn_post=row(g_ffn_post))


def _trunk_tail(x2, oa, ob, z, w):
    t = x2.shape[0]
    tm = _largest_divisor(t, (512, 256, 128, 64, 32, 16, 8))
    x1 = _mix_out(x2, oa, ob, z, w["g_gla_sub"], w["w_out"], w["g_mix_post"], tm=tm)
    dff = w["w_gate"].shape[1]
    tf = dff // 2 if (dff // 2) % LANES == 0 else dff
    tmf = _largest_divisor(t, (1024, 512, 256, 128, 64, 32, 16, 8))
    return _ffn(x1, w["g_ffn_pre"], w["w_gate"], w["w_up"], w["w_down"], w["g_ffn_post"], tm=tmf, tf=tf)


def kernel(x_prompt, x_sample, cache_k, cache_v, state_gla, page_table, g_mix_pre, w_in, w_gk, b_gk,
           lambda_q1, lambda_k1, lambda_q2, lambda_k2, g_attn_sub, g_gla_sub, w_out, g_mix_post,
           g_ffn_pre, w_gate, w_up, w_down, g_ffn_post):
    depth = w_in.shape[0]
    bsz, seq, d = x_prompt.shape
    dbsz, dseq, _ = x_sample.shape
    heads = d // A_DV
    page = cache_k.shape[2]
    n_pool = cache_k.shape[1]
    n_pages = page_table.shape[1]
    dk = d // (2 * B_HEADS)
    dv = d // B_HEADS
    assert bsz == 1, "prompt kernels take one sequence"
    pad_seq = -(-dseq // SUBLANES) * SUBLANES

    x_p = x_prompt.reshape(seq, d)
    x_s = jnp.pad(x_sample, ((0, 0), (0, pad_seq - dseq), (0, 0))).reshape(dbsz * pad_seq, d)

    outs = ([], [], [], [], [], [])
    for l in range(depth):
        lam_init = 0.8 - 0.6 * math.exp(-0.3 * l)
        w = _prep_weights(g_mix_pre[l], w_in[l], w_gk[l], b_gk[l], g_attn_sub[l], g_gla_sub[l], w_out[l],
                          g_mix_post[l], g_ffn_pre[l], w_gate[l], w_up[l], w_down[l], g_ffn_post[l])
        lams = [v[l].reshape(1, A_DQK).astype(F32) for v in (lambda_q1, lambda_k1, lambda_q2, lambda_k2)]
        attn_proj = functools.partial(_attn_proj, g=w["g_mix_pre"], wq_t=w["wq_t"], wv_t=w["wv_t"],
                                      wk=w["wk"], wv=w["wv"])
        mix_proj = functools.partial(_mix_proj, g=w["g_mix_pre"], wg=w["wg"], wgk=w["wgk"], bgk=w["bgk"],
                                     wm=w["wm"])

        qt, vt, kb, k_p, v_p = attn_proj(x_p, tm=_largest_divisor(seq, (512, 256, 128)))
        z, la = mix_proj(x_p, tm=_largest_divisor(seq, (1024, 512, 256, 128)))
        blk = _largest_divisor(seq, (512, 256, 128))
        oa = _attn_prompt(qt, kb, vt, *lams, w["g_attn_col"], bq=blk, bk=blk, lam_init=lam_init)
        chunk = _largest_divisor(seq, (128, 64, 32, 16, 8))
        tc = _largest_divisor(seq, (1024, 512, 256, 128, 64, 32, 16, 8))
        ob, s_p = _gla_prompt(z, la, d=d, tc=tc, chunk=chunk, sub=min(16, chunk))
        x_p = _trunk_tail(x_p, oa, ob, z, w)
        outs[0].append(k_p.reshape(bsz, seq, heads, A_DV))
        outs[1].append(v_p.reshape(bsz, seq, heads, A_DV))
        outs[2].append(s_p.reshape(bsz, B_HEADS, dk, dv))

        ts = dbsz * pad_seq
        tms = _largest_divisor(ts, (256, 128))
        qts, _, _, k_s, v_s = attn_proj(x_s, tm=tms)
        zs, las = mix_proj(x_s, tm=tms)
        zs3 = zs.reshape(dbsz, pad_seq, zs.shape[1])
        q_s = qts.T.reshape(dbsz, pad_seq, d)
        k_s = k_s.reshape(dbsz, pad_seq * heads, A_DV)
        v_s = v_s.reshape(dbsz, pad_seq * heads, A_DV)
        n_group = _largest_divisor(n_pages, (16, 8, 4, 2, 1))
        oas = _attn_decode(page_table, q_s, k_s, v_s, cache_k[l].reshape(n_pool, page * heads, A_DV),
                           cache_v[l].reshape(n_pool, page * heads, A_DV), *lams, w["g_attn_row"],
                           n_group=n_group, lam_init=lam_init)
        obs, s_s = _gla_sample(zs3, las.reshape(dbsz, pad_seq, las.shape[1]), state_gla[l], d=d, n_valid=dseq)
        x_s = _trunk_tail(x_s, oas.reshape(ts, d), obs.reshape(ts, d), zs, w)
        outs[3].append(k_s.reshape(dbsz, pad_seq, heads, A_DV)[:, :dseq])
        outs[4].append(v_s.reshape(dbsz, pad_seq, heads, A_DV)[:, :dseq])
        outs[5].append(s_s)

    y_p = x_p.reshape(bsz, seq, d)
    y_s = x_s.reshape(dbsz, pad_seq, d)[:, :dseq]
    k_p, v_p, g_p, k_s, v_s, g_s = (jnp.stack(o) for o in outs)
    return (y_p, y_s, k_p, v_p, g_p, k_s, v_s, g_s)
```

```python
import functools
import math

import jax
import jax.numpy as jnp
from jax import lax
from jax.experimental import pallas as pl
from jax.experimental.pallas import tpu as pltpu

A_DQK = 64
A_DV = 2 * A_DQK
B_HEADS = 4
GATE_TAU = 16.0
EPS = 1e-6

LANES = 128
SUBLANES = 8
BF16_SUBLANES = 16
VMEM_LIMIT_BYTES = 56 * 1024 * 1024

NEG_BIG = -1e30
LOG2_E = math.log2(math.e)
ATTN_SCORE_BUFFERS = 3
ATTN_SCORE_LOOKAHEAD = 1
F32 = jnp.float32
BF16 = jnp.bfloat16


def _params(*semantics):
    return pltpu.CompilerParams(dimension_semantics=semantics, vmem_limit_bytes=VMEM_LIMIT_BYTES)


def _rms(x, g):
    return x * lax.rsqrt(jnp.mean(x * x, axis=-1, keepdims=True) + EPS) * g


def _lambda(lq1, lk1, lq2, lk2, lam_init):
    s1 = jnp.sum(lq1 * lk1, axis=-1, keepdims=True)
    s2 = jnp.sum(lq2 * lk2, axis=-1, keepdims=True)
    return jnp.exp(s1) - jnp.exp(s2) + lam_init


def _attn_proj_kernel(x_ref, g_ref, wq_t_ref, wv_t_ref, wk_ref, wv_ref,
                      qt_ref, vt_ref, kb_ref, ko_ref, vo_ref, *, q_scale, heads):
    tm = x_ref.shape[0]
    h = _rms(x_ref[...], g_ref[...]).astype(BF16)
    nt = (((1,), (1,)), ((), ()))
    qt = lax.dot_general(wq_t_ref[...], h, nt, preferred_element_type=F32)
    qt_ref[...] = (qt * q_scale).astype(BF16)
    vt = lax.dot_general(wv_t_ref[...], h, nt, preferred_element_type=F32)
    vt_ref[...] = vt.astype(BF16)
    k = jnp.dot(h, wk_ref[...], preferred_element_type=F32)
    kb_ref[...] = k.astype(BF16)
    v = jnp.dot(h, wv_ref[...], preferred_element_type=F32)
    for hd in range(heads):
        ko_ref[pl.ds(hd, tm, stride=heads), :] = k[:, hd * A_DV:(hd + 1) * A_DV]
        vo_ref[pl.ds(hd, tm, stride=heads), :] = v[:, hd * A_DV:(hd + 1) * A_DV]


def _attn_proj(x, g, wq_t, wv_t, wk, wv, *, tm):
    t, d = x.shape
    heads = d // A_DV
    const = lambda i: (0, 0)
    return pl.pallas_call(
        functools.partial(_attn_proj_kernel, q_scale=A_DQK ** -0.5 * LOG2_E, heads=heads),
        grid=(t // tm,),
        in_specs=[
            pl.BlockSpec((tm, d), lambda i: (i, 0)),
            pl.BlockSpec((1, d), const),
            pl.BlockSpec((d, d), const),
            pl.BlockSpec((d, d), const),
            pl.BlockSpec((d, d), const),
            pl.BlockSpec((d, d), const),
        ],
        out_specs=[
            pl.BlockSpec((d, tm), lambda i: (0, i)),
            pl.BlockSpec((d, tm), lambda i: (0, i)),
            pl.BlockSpec((tm, d), lambda i: (i, 0)),
            pl.BlockSpec((tm * heads, A_DV), lambda i: (i, 0)),
            pl.BlockSpec((tm * heads, A_DV), lambda i: (i, 0)),
        ],
        out_shape=[
            jax.ShapeDtypeStruct((d, t), BF16),
            jax.ShapeDtypeStruct((d, t), BF16),
            jax.ShapeDtypeStruct((t, d), BF16),
            jax.ShapeDtypeStruct((t * heads, A_DV), F32),
            jax.ShapeDtypeStruct((t * heads, A_DV), F32),
        ],
        compiler_params=_params("parallel"),
        name="attn_proj",
    )(x, g, wq_t, wv_t, wk, wv)


def _mix_proj_kernel(x_ref, g_ref, wg_ref, wgk_ref, bgk_ref, wm_ref, z_ref, la_ref, h_ref):
    @pl.when(pl.program_id(1) == 0)
    def _():
        h = _rms(x_ref[...], g_ref[...]).astype(BF16)
        h_ref[...] = h
        g_lr = jnp.dot(h, wg_ref[...], preferred_element_type=F32)
        gk = jnp.dot(g_lr.astype(BF16), wgk_ref[...], preferred_element_type=F32) + bgk_ref[...]
        la_ref[...] = jax.nn.log_sigmoid(gk) / GATE_TAU

    z_ref[...] = jnp.dot(h_ref[...], wm_ref[...], preferred_element_type=F32)


def _mix_proj(x, g, wg, wgk, bgk, wm, *, tm):
    t, d = x.shape
    n_tiles = wm.shape[1] // d
    wk = wgk.shape[1]
    const = lambda i, j: (0, 0)
    return pl.pallas_call(
        _mix_proj_kernel,
        grid=(t // tm, n_tiles),
        in_specs=[
            pl.BlockSpec((tm, d), lambda i, j: (i, 0)),
            pl.BlockSpec((1, d), const),
            pl.BlockSpec((d, LANES), const),
            pl.BlockSpec((LANES, wk), const),
            pl.BlockSpec((1, wk), const),
            pl.BlockSpec((d, d), lambda i, j: (0, j)),
        ],
        out_specs=[
            pl.BlockSpec((tm, d), lambda i, j: (i, j)),
            pl.BlockSpec((tm, wk), lambda i, j: (i, 0)),
        ],
        out_shape=[
            jax.ShapeDtypeStruct((t, n_tiles * d), F32),
            jax.ShapeDtypeStruct((t, wk), F32),
        ],
        scratch_shapes=[pltpu.VMEM((tm, d), BF16)],
        compiler_params=_params("parallel", "arbitrary"),
        name="mix_proj",
    )(x, g, wg, wgk, bgk, wm)


def _attn_prompt_kernel(qt_ref, k_ref, vt_ref, lq1_ref, lk1_ref, lq2_ref, lk2_ref, gcol_ref,
                        o_ref, qs_ref, s_refs, c_refs, m_ref, l_ref, acc_ref, *, bq, bk, lam_init):
    n_buf = len(s_refs)
    ratio = bq // bk
    i = pl.program_id(1)
    q = qt_ref[...]
    row = lax.broadcasted_iota(jnp.int32, q.shape, 0)
    zero = jnp.zeros_like(q)
    qs_ref[0] = jnp.where(row < A_DQK, q, zero)
    qs_ref[1] = jnp.where(row >= A_DQK, q, zero)

    m_ref[...] = jnp.full(m_ref.shape, NEG_BIG, F32)
    l_ref[...] = jnp.zeros(l_ref.shape, F32)
    acc_ref[...] = jnp.zeros(acc_ref.shape, F32)

    def scores(kb, slot):
        start = pl.multiple_of(kb * bk, bk)
        k = k_ref[pl.ds(start, bk), :]
        for mi in range(2):
            s = jnp.dot(k, qs_ref[mi], preferred_element_type=F32)
            s_refs[slot][mi] = s
            c_refs[slot][mi] = jnp.max(s, axis=0, keepdims=True)

    def absorb(kb, slot, diag):
        start = pl.multiple_of(kb * bk, bk)
        vt = vt_ref[:, pl.ds(start, bk)]
        vt1 = jnp.concatenate([vt, jnp.ones((BF16_SUBLANES, bk), BF16)], axis=0)
        for mi in range(2):
            s = s_refs[slot][mi]
            if diag is not None:
                kpos = lax.broadcasted_iota(jnp.int32, s.shape, 0) + diag * bk
                qpos = lax.broadcasted_iota(jnp.int32, s.shape, 1)
                s = jnp.where(kpos <= qpos, s, NEG_BIG)
            m_old = m_ref[mi]
            cmax = c_refs[slot][mi] if diag is None else jnp.max(s, axis=0, keepdims=True)
            m_new = jnp.maximum(m_old, cmax)
            alpha = jnp.exp2(m_old - m_new)
            p = jnp.exp2(s - m_new).astype(BF16)
            upd = jnp.dot(vt1, p, preferred_element_type=F32)
            l_ref[mi] = alpha * l_ref[mi] + upd[A_DV:A_DV + 1]
            acc_ref[mi] = alpha * acc_ref[mi] + upd[:A_DV]
            m_ref[mi] = m_new

    ahead = ATTN_SCORE_LOOKAHEAD
    assert ahead <= ratio and ahead < n_buf
    for j in range(ahead):
        scores(j, j)

    n_plain = i * ratio
    n_full = n_plain // n_buf

    def full_round(r, carry):
        kb = r * n_buf
        for j in range(n_buf):
            scores(kb + j + ahead, (j + ahead) % n_buf)
            absorb(kb + j, j, None)
        return carry

    lax.fori_loop(0, n_full, full_round, 0)

    kb0 = n_full * n_buf
    rest = n_plain + ratio - kb0
    for count in range(ratio, ratio + n_buf):
        def tail(count=count):
            for j in range(count):
                if j + ahead < count:
                    scores(kb0 + j + ahead, (j + ahead) % n_buf)
                absorb(kb0 + j, j % n_buf, None if j < count - ratio else j - (count - ratio))
        pl.when(rest == count)(tail)

    lam = _lambda(lq1_ref[...], lk1_ref[...], lq2_ref[...], lk2_ref[...], lam_init)
    o = acc_ref[0] / l_ref[0] - lam * (acc_ref[1] / l_ref[1])
    ms = jnp.mean(o * o, axis=0, keepdims=True)
    y = o * lax.rsqrt(ms + EPS) * gcol_ref[...] * (1.0 - lam_init)
    o_ref[...] = y.T


def _attn_prompt(qt, kb, vt, lq1, lk1, lq2, lk2, gcol, *, bq, bk, lam_init):
    d, t = qt.shape
    heads = d // A_DV
    vec = pl.BlockSpec((1, A_DQK), lambda h, i: (0, 0))
    return pl.pallas_call(
        functools.partial(_attn_prompt_kernel, bq=bq, bk=bk, lam_init=lam_init),
        grid=(heads, t // bq),
        in_specs=[
            pl.BlockSpec((A_DV, bq), lambda h, i: (h, i)),
            pl.BlockSpec((t, A_DV), lambda h, i: (0, h)),
            pl.BlockSpec((A_DV, t), lambda h, i: (h, 0)),
            vec, vec, vec, vec,
            pl.BlockSpec((A_DV, 1), lambda h, i: (0, 0)),
        ],
        out_specs=pl.BlockSpec((bq, A_DV), lambda h, i: (i, h)),
        out_shape=jax.ShapeDtypeStruct((t, d), F32),
        scratch_shapes=[
            pltpu.VMEM((2, A_DV, bq), BF16),
            tuple(pltpu.VMEM((2, bk, bq), F32) for _ in range(ATTN_SCORE_BUFFERS)),
            tuple(pltpu.VMEM((2, 1, bq), F32) for _ in range(ATTN_SCORE_BUFFERS)),
            pltpu.VMEM((2, 1, bq), F32),
            pltpu.VMEM((2, 1, bq), F32),
            pltpu.VMEM((2, A_DV, bq), F32),
        ],
        compiler_params=_params("parallel", "arbitrary"),
        name="attn_prompt",
    )(qt, kb, vt, lq1, lk1, lq2, lk2, gcol)


def _attn_decode_kernel(pt_ref, q_ref, kn_ref, vn_ref, lq1_ref, lk1_ref, lq2_ref, lk2_ref, g_ref,
                        *rest, n_group, n_tok, lam_init):
    k_refs = rest[:n_group]
    v_refs = rest[n_group:2 * n_group]
    o_ref, qbd_ref, m_ref, l_ref, acc_ref, kpage_ref, vpage_ref, kcat_ref, vcat_ref = rest[2 * n_group:]
    j = pl.program_id(1)
    rows, d = qbd_ref.shape
    page = kpage_ref.shape[0]
    heads = d // A_DV

    def update(k, v, mask):
        nt = (((1,), (1,)), ((), ()))
        s = lax.dot_general(qbd_ref[...], k, nt, preferred_element_type=F32)
        if mask is not None:
            s = jnp.where(mask, s, NEG_BIG)
        m_old = m_ref[...]
        m_new = jnp.maximum(m_old, jnp.max(s, axis=1, keepdims=True))
        alpha = jnp.exp2(m_old - m_new)
        p = jnp.exp2(s - m_new)
        l_ref[...] = alpha * l_ref[...] + jnp.sum(p, axis=1, keepdims=True)
        acc_ref[...] = alpha * acc_ref[...] + jnp.dot(p.astype(BF16), v, preferred_element_type=F32)
        m_ref[...] = m_new

    def gather_heads(ref, n_keys):
        cols = [ref[0, pl.ds(h, n_keys, stride=heads), :] for h in range(heads)]
        return jnp.concatenate(cols, axis=1).astype(BF16)

    @pl.when(j == 0)
    def _():
        q = q_ref[0]
        qrep = jnp.concatenate([q] * (rows // n_tok), axis=0)
        r = lax.broadcasted_iota(jnp.int32, (rows, d), 0)
        c = lax.broadcasted_iota(jnp.int32, (rows, d), 1)
        qbd_ref[...] = jnp.where(r // n_tok == c // A_DQK, qrep, jnp.zeros_like(qrep))
        m_ref[...] = jnp.full(m_ref.shape, NEG_BIG, F32)
        l_ref[...] = jnp.zeros(l_ref.shape, F32)
        acc_ref[...] = jnp.zeros(acc_ref.shape, F32)
        kpage_ref[...] = jnp.zeros(kpage_ref.shape, BF16)
        vpage_ref[...] = jnp.zeros(vpage_ref.shape, BF16)
        kpage_ref[0:n_tok, :] = gather_heads(kn_ref, n_tok)
        vpage_ref[0:n_tok, :] = gather_heads(vn_ref, n_tok)
        tq = lax.broadcasted_iota(jnp.int32, (rows, page), 0) % n_tok
        ts = lax.broadcasted_iota(jnp.int32, (rows, page), 1)
        update(kpage_ref[...], vpage_ref[...], ts <= tq)

    def load_pages(refs, dst_ref):
        for g_idx, ref in enumerate(refs):
            dst_ref[g_idx * page:(g_idx + 1) * page, :] = gather_heads(ref, page)
        return dst_ref[...]

    update(load_pages(k_refs, kcat_ref), load_pages(v_refs, vcat_ref), None)

    @pl.when(j == pl.num_programs(1) - 1)
    def _():
        lam = _lambda(lq1_ref[...], lk1_ref[...], lq2_ref[...], lk2_ref[...], lam_init)
        on = acc_ref[...] / l_ref[...]
        g = g_ref[...]
        for h in range(heads):
            r0 = h * 2 * n_tok
            c0 = h * A_DV
            o = on[r0:r0 + n_tok, c0:c0 + A_DV] - lam * on[r0 + n_tok:r0 + 2 * n_tok, c0:c0 + A_DV]
            o_ref[0, :, c0:c0 + A_DV] = _rms(o, g) * (1.0 - lam_init)


def _attn_decode(page_table, q, k_new, v_new, cache_k, cache_v, lq1, lk1, lq2, lk2, g, *, n_group, lam_init):
    bsz, n_tok, d = q.shape
    n_pages = page_table.shape[1]
    heads = d // A_DV
    page = cache_k.shape[1] // heads
    rows = (d // A_DQK) * n_tok
    vec = pl.BlockSpec((1, A_DQK), lambda b, j, pt: (0, 0))

    def page_spec(g_idx):
        return pl.BlockSpec((1, page * heads, A_DV), lambda b, j, pt: (pt[b, j * n_group + g_idx], 0, 0))

    grid_spec = pltpu.PrefetchScalarGridSpec(
        num_scalar_prefetch=1,
        grid=(bsz, n_pages // n_group),
        in_specs=[
            pl.BlockSpec((1, n_tok, d), lambda b, j, pt: (b, 0, 0)),
            pl.BlockSpec((1, n_tok * heads, A_DV), lambda b, j, pt: (b, 0, 0)),
            pl.BlockSpec((1, n_tok * heads, A_DV), lambda b, j, pt: (b, 0, 0)),
            vec, vec, vec, vec,
            pl.BlockSpec((1, A_DV), lambda b, j, pt: (0, 0)),
        ] + [page_spec(g_idx) for g_idx in range(n_group)] * 2,
        out_specs=pl.BlockSpec((1, n_tok, d), lambda b, j, pt: (b, 0, 0)),
        scratch_shapes=[
            pltpu.VMEM((rows, d), BF16),
            pltpu.VMEM((rows, 1), F32),
            pltpu.VMEM((rows, 1), F32),
            pltpu.VMEM((rows, d), F32),
            pltpu.VMEM((page, d), BF16),
            pltpu.VMEM((page, d), BF16),
            pltpu.VMEM((n_group * page, d), BF16),
            pltpu.VMEM((n_group * page, d), BF16),
        ],
    )
    return pl.pallas_call(
        functools.partial(_attn_decode_kernel, n_group=n_group, n_tok=n_tok, lam_init=lam_init),
        grid_spec=grid_spec,
        out_shape=jax.ShapeDtypeStruct((bsz, n_tok, d), F32),
        compiler_params=_params("parallel", "arbitrary"),
        name="attn_decode",
    )(page_table, q, k_new, v_new, lq1, lk1, lq2, lk2, g, *([cache_k] * n_group), *([cache_v] * n_group))


def _cumsum_rows(la):
    c = la.shape[0]
    t_i = lax.broadcasted_iota(jnp.int32, (c, c), 0)
    s_i = lax.broadcasted_iota(jnp.int32, (c, c), 1)
    tri = (s_i <= t_i).astype(F32)
    return jnp.dot(tri, la, precision=lax.Precision.HIGHEST, preferred_element_type=F32)


def _gla_chunk(q, k, v, b, st, *, sub, scale):
    c, dk = q.shape
    q = q * scale
    b = b * LOG2_E
    nt = (((1,), (1,)), ((), ()))

    o = lax.dot_general((q * jnp.exp2(b)).astype(BF16), st.astype(BF16), nt, preferred_element_type=F32)

    lane = lax.broadcasted_iota(jnp.int32, (sub, c), 1)
    trow = lax.broadcasted_iota(jnp.int32, (sub, c), 0)
    a_rows = []
    for blk in range(c // sub):
        r0 = blk * sub
        qi = q[r0:r0 + sub]
        bi = b[r0:r0 + sub]
        a_blk = jnp.zeros((sub, c), F32)
        if blk > 0:
            ref = b[r0:r0 + 1]
            qs = qi * jnp.exp2(bi - ref)
            ks = k * jnp.exp2(jnp.minimum(ref - b, 0.0))
            a_off = lax.dot_general(qs.astype(BF16), ks.astype(BF16), nt, preferred_element_type=F32)
            a_blk = jnp.where(lane < r0, a_off, a_blk)
        for s in range(sub):
            ks_row = k[r0 + s:r0 + s + 1]
            bs_row = b[r0 + s:r0 + s + 1]
            x = qi * ks_row * jnp.exp2(bi - bs_row)
            col = jnp.sum(x, axis=1, keepdims=True)
            a_blk = jnp.where((lane == r0 + s) & (trow >= s), col, a_blk)
        a_rows.append(a_blk)
    a = a_rows[0] if len(a_rows) == 1 else jnp.concatenate(a_rows, axis=0)
    o = o + jnp.dot(a.astype(BF16), v.astype(BF16), preferred_element_type=F32)

    bl = b[c - 1:c]
    kd = k * jnp.exp2(bl - b)
    tn = (((0,), (0,)), ((), ()))
    st_new = st * jnp.exp2(bl) + lax.dot_general(v.astype(BF16), kd.astype(BF16), tn, preferred_element_type=F32)
    return o, st_new


def _gla_prompt_kernel(qk_ref, v_ref, la_ref, x_ref, oa_ref, r_ref, ga_ref, gb_ref, gsub_ref, wo_ref, gpost_ref,
                       y_ref, s_ref, st_ref, o_ref, *, chunk, sub, scale):
    t = pl.program_id(0)
    heads, dv, dk = st_ref.shape

    @pl.when(t == 0)
    def _():
        st_ref[...] = jnp.zeros(st_ref.shape, F32)

    def body(ci, carry):
        r0 = pl.multiple_of(ci * chunk, chunk)
        rows = pl.ds(r0, chunk)
        b_all = _cumsum_rows(la_ref[rows, :])
        for h in range(heads):
            o, st_new = _gla_chunk(qk_ref[rows, h * dk:(h + 1) * dk],
                                   qk_ref[rows, (heads + h) * dk:(heads + h + 1) * dk],
                                   v_ref[rows, h * dv:(h + 1) * dv],
                                   b_all[:, h * dk:(h + 1) * dk], st_ref[h], sub=sub, scale=scale)
            o_ref[rows, h * dv:(h + 1) * dv] = o
            st_ref[h] = st_new
        return carry

    lax.fori_loop(0, qk_ref.shape[0] // chunk, body, 0)

    @pl.when(t == pl.num_programs(0) - 1)
    def _():
        for h in range(heads):
            s_ref[h] = st_ref[h].T

    y_ref[...] = _mix(x_ref[...], oa_ref[...], o_ref[...], r_ref[...], ga_ref[...], gb_ref[...],
                      gsub_ref[...], wo_ref[...], gpost_ref[...])


def _gla_prompt(x, oa, z, la, gsub, wo, gpost, *, tc, chunk, sub):
    t, d = x.shape
    dk = d // (2 * B_HEADS)
    dv = d // B_HEADS
    const = lambda i: (0, 0)
    tile = lambda j: pl.BlockSpec((tc, d), lambda i: (i, j))
    return pl.pallas_call(
        functools.partial(_gla_prompt_kernel, chunk=chunk, sub=sub, scale=dk ** -0.5),
        grid=(t // tc,),
        in_specs=[
            tile(0), tile(1),
            pl.BlockSpec((tc, B_HEADS * dk), lambda i: (i, 0)),
            tile(0), tile(0), tile(2), tile(3), tile(4),
            pl.BlockSpec((1, gsub.shape[1]), const),
            pl.BlockSpec((d, d), const),
            pl.BlockSpec((1, d), const),
        ],
        out_specs=[
            pl.BlockSpec((tc, d), lambda i: (i, 0)),
            pl.BlockSpec((B_HEADS, dk, dv), lambda i: (0, 0, 0)),
        ],
        out_shape=[
            jax.ShapeDtypeStruct((t, d), F32),
            jax.ShapeDtypeStruct((B_HEADS, dk, dv), F32),
        ],
        scratch_shapes=[pltpu.VMEM((B_HEADS, dv, dk), F32), pltpu.VMEM((tc, d), F32)],
        compiler_params=_params("arbitrary"),
        name="gla_prompt",
    )(z, z, la, x, oa, z, z, z, gsub, wo, gpost)


def _gla_sample_kernel(qk_ref, v_ref, la_ref, s0_ref, o_ref, s_ref, *, n_valid, scale):
    c = qk_ref.shape[1]
    heads, dk, dv = s0_ref.shape[1:]
    valid = lax.broadcasted_iota(jnp.int32, (c, 1), 0) < n_valid
    b_all = _cumsum_rows(jnp.where(valid, la_ref[0], 0.0))
    for h in range(heads):
        k = jnp.where(valid, qk_ref[0, :, (heads + h) * dk:(heads + h + 1) * dk], 0.0)
        o, st_new = _gla_chunk(qk_ref[0, :, h * dk:(h + 1) * dk], k, v_ref[0, :, h * dv:(h + 1) * dv],
                               b_all[:, h * dk:(h + 1) * dk], s0_ref[0, h].T, sub=c, scale=scale)
        o_ref[0, :, h * dv:(h + 1) * dv] = o
        s_ref[0, h] = st_new.T


def _gla_sample(z3, la3, s0, *, d, n_valid):
    bsz, c, _ = z3.shape
    state_spec = pl.BlockSpec((1,) + s0.shape[1:], lambda b: (b, 0, 0, 0))
    return pl.pallas_call(
        functools.partial(_gla_sample_kernel, n_valid=n_valid, scale=s0.shape[2] ** -0.5),
        grid=(bsz,),
        in_specs=[
            pl.BlockSpec((1, c, d), lambda b: (b, 0, 0)),
            pl.BlockSpec((1, c, d), lambda b: (b, 0, 1)),
            pl.BlockSpec((1, c, la3.shape[2]), lambda b: (b, 0, 0)),
            state_spec,
        ],
        out_specs=[pl.BlockSpec((1, c, d), lambda b: (b, 0, 0)), state_spec],
        out_shape=[
            jax.ShapeDtypeStruct((bsz, c, d), F32),
            jax.ShapeDtypeStruct(s0.shape, F32),
        ],
        compiler_params=_params("parallel"),
        name="gla_sample",
    )(z3, z3, la3, s0)


def _mix(x, oa, ob, r, ga, gb, gsub, wo, gpost):
    dv = gsub.shape[1]
    ob_n = jnp.concatenate(
        [_rms(ob[:, h * dv:(h + 1) * dv], gsub) for h in range(ob.shape[1] // dv)], axis=1)
    ob_n = ob_n * jax.nn.silu(r)
    merged = jax.nn.sigmoid(ga) * oa + jax.nn.sigmoid(gb) * ob_n
    y = jnp.dot(merged.astype(BF16), wo, preferred_element_type=F32)
    return x + _rms(y, gpost)


def _mix_out_kernel(x_ref, oa_ref, ob_ref, r_ref, ga_ref, gb_ref, gsub_ref, wo_ref, gpost_ref, y_ref):
    y_ref[...] = _mix(x_ref[...], oa_ref[...], ob_ref[...], r_ref[...], ga_ref[...], gb_ref[...],
                      gsub_ref[...], wo_ref[...], gpost_ref[...])


def _mix_out(x, oa, ob, z, gsub, wo, gpost, *, tm):
    t, d = x.shape
    row = lambda i: (i, 0)
    const = lambda i: (0, 0)
    return pl.pallas_call(
        _mix_out_kernel,
        grid=(t // tm,),
        in_specs=[
            pl.BlockSpec((tm, d), row),
            pl.BlockSpec((tm, d), row),
            pl.BlockSpec((tm, d), row),
            pl.BlockSpec((tm, d), lambda i: (i, 2)),
            pl.BlockSpec((tm, d), lambda i: (i, 3)),
            pl.BlockSpec((tm, d), lambda i: (i, 4)),
            pl.BlockSpec((1, gsub.shape[1]), const),
            pl.BlockSpec((d, d), const),
            pl.BlockSpec((1, d), const),
        ],
        out_specs=pl.BlockSpec((tm, d), row),
        out_shape=jax.ShapeDtypeStruct((t, d), F32),
        compiler_params=_params("parallel"),
        name="mix_out",
    )(x, oa, ob, z, z, z, gsub, wo, gpost)


def _ffn_kernel(x_ref, gpre_ref, wg_ref, wu_ref, wd_ref, gpost_ref, y_ref, h_ref, acc_ref):
    j = pl.program_id(1)

    @pl.when(j == 0)
    def _():
        h_ref[...] = _rms(x_ref[...], gpre_ref[...]).astype(BF16)
        acc_ref[...] = jnp.zeros(acc_ref.shape, F32)

    h = h_ref[...]
    gate = jnp.dot(h, wg_ref[...], preferred_element_type=F32)
    up = jnp.dot(h, wu_ref[...], preferred_element_type=F32)
    act = (jax.nn.silu(gate) * up).astype(BF16)
    acc_ref[...] += jnp.dot(act, wd_ref[...], preferred_element_type=F32)

    @pl.when(j == pl.num_programs(1) - 1)
    def _():
        y_ref[...] = x_ref[...] + _rms(acc_ref[...], gpost_ref[...])


def _ffn(x, gpre, wg, wu, wd, gpost, *, tm, tf):
    t, d = x.shape
    dff = wg.shape[1]
    row = lambda i, j: (i, 0)
    const = lambda i, j: (0, 0)
    return pl.pallas_call(
        _ffn_kernel,
        grid=(t // tm, dff // tf),
        in_specs=[
            pl.BlockSpec((tm, d), row),
            pl.BlockSpec((1, d), const),
            pl.BlockSpec((d, tf), lambda i, j: (0, j)),
            pl.BlockSpec((d, tf), lambda i, j: (0, j)),
            pl.BlockSpec((tf, d), lambda i, j: (j, 0)),
            pl.BlockSpec((1, d), const),
        ],
        out_specs=pl.BlockSpec((tm, d), row),
        out_shape=jax.ShapeDtypeStruct((t, d), F32),
        scratch_shapes=[pltpu.VMEM((tm, d), BF16), pltpu.VMEM((tm, d), F32)],
        compiler_params=_params("parallel", "arbitrary"),
        name="ffn",
    )(x, gpre, wg, wu, wd, gpost)


def _largest_divisor(n, candidates):
    for c in candidates:
        if n % c == 0:
            return c
    raise ValueError(f"no supported tile for extent {n}")


def _prep_weights(g_mix_pre, w_in, w_gk, b_gk, g_attn_sub, g_gla_sub, w_out, g_mix_post,
                  g_ffn_pre, w_gate, w_up, w_down, g_ffn_post):
    d = w_in.shape[0]
    rank = w_gk.shape[0]
    a_w, wk, wv = d, d // 2, d
    off = [0]
    for s in (a_w, a_w, a_w, wk, wk, wv, rank, wv, d, d):
        off.append(off[-1] + s)
    col = lambda a, b: w_in[:, off[a]:off[b]]
    wq_t = col(0, 1).T.astype(BF16)
    wv_t = col(2, 3).T.astype(BF16)
    wm = jnp.concatenate([col(3, 5), col(5, 6), col(7, 8), col(8, 9), col(9, 10)], axis=1).astype(BF16)
    wg = jnp.pad(col(6, 7), ((0, 0), (0, LANES - rank))).astype(BF16)
    wgk = jnp.pad(w_gk, ((0, LANES - rank), (0, 0))).astype(BF16)
    row = lambda v: v.reshape(1, -1).astype(F32)
    return dict(
        g_mix_pre=row(g_mix_pre), wq_t=wq_t, wv_t=wv_t, wk=col(1, 2).astype(BF16), wv=col(2, 3).astype(BF16),
        wm=wm, wg=wg, wgk=wgk, bgk=row(b_gk),
        g_attn_row=row(g_attn_sub), g_attn_col=g_attn_sub.reshape(-1, 1).astype(F32),
        g_gla_sub=row(g_gla_sub), w_out=w_out.astype(BF16), g_mix_post=row(g_mix_post),
        g_ffn_pre=row(g_ffn_pre), w_gate=w_gate.astype(BF16), w_up=w_up.astype(BF16),
        w_down=w_down.astype(BF16), g_ffn_post=row(g_ffn_post))


def _ffn_block(x1, w):
    t = x1.shape[0]
    dff = w["w_gate"].shape[1]
    tf = dff // 2 if (dff // 2) % LANES == 0 else dff
    tmf = _largest_divisor(t, (1024, 512, 256, 128, 64, 32, 16, 8))
    return _ffn(x1, w["g_ffn_pre"], w["w_gate"], w["w_up"], w["w_down"], w["g_ffn_post"], tm=tmf, tf=tf)


def kernel(x_prompt, x_sample, cache_k, cache_v, state_gla, page_table, g_mix_pre, w_in, w_gk, b_gk,
           lambda_q1, lambda_k1, lambda_q2, lambda_k2, g_attn_sub, g_gla_sub, w_out, g_mix_post,
           g_ffn_pre, w_gate, w_up, w_down, g_ffn_post):
    depth = w_in.shape[0]
    bsz, seq, d = x_prompt.shape
    dbsz, dseq, _ = x_sample.shape
    heads = d // A_DV
    page = cache_k.shape[2]
    n_pool = cache_k.shape[1]
    n_pages = page_table.shape[1]
    dk = d // (2 * B_HEADS)
    dv = d // B_HEADS
    assert bsz == 1, "prompt kernels take one sequence"
    pad_seq = -(-dseq // SUBLANES) * SUBLANES

    x_p = x_prompt.reshape(seq, d)
    x_s = jnp.pad(x_sample, ((0, 0), (0, pad_seq - dseq), (0, 0))).reshape(dbsz * pad_seq, d)

    outs = ([], [], [], [], [], [])
    for l in range(depth):
        lam_init = 0.8 - 0.6 * math.exp(-0.3 * l)
        w = _prep_weights(g_mix_pre[l], w_in[l], w_gk[l], b_gk[l], g_attn_sub[l], g_gla_sub[l], w_out[l],
                          g_mix_post[l], g_ffn_pre[l], w_gate[l], w_up[l], w_down[l], g_ffn_post[l])
        lams = [v[l].reshape(1, A_DQK).astype(F32) for v in (lambda_q1, lambda_k1, lambda_q2, lambda_k2)]
        attn_proj = functools.partial(_attn_proj, g=w["g_mix_pre"], wq_t=w["wq_t"], wv_t=w["wv_t"],
                                      wk=w["wk"], wv=w["wv"])
        mix_proj = functools.partial(_mix_proj, g=w["g_mix_pre"], wg=w["wg"], wgk=w["wgk"], bgk=w["bgk"],
                                     wm=w["wm"])

        qt, vt, kb, k_p, v_p = attn_proj(x_p, tm=_largest_divisor(seq, (512, 256, 128)))
        z, la = mix_proj(x_p, tm=_largest_divisor(seq, (1024, 512, 256, 128)))
        blk = _largest_divisor(seq, (512, 256, 128))
        oa = _attn_prompt(qt, kb, vt, *lams, w["g_attn_col"], bq=blk, bk=blk, lam_init=lam_init)
        chunk = _largest_divisor(seq, (128, 64, 32, 16, 8))
        tc = _largest_divisor(seq, (512, 256, 128, 64, 32, 16, 8))
        x_p, s_p = _gla_prompt(x_p, oa, z, la, w["g_gla_sub"], w["w_out"], w["g_mix_post"],
                               tc=tc, chunk=chunk, sub=min(16, chunk))
        x_p = _ffn_block(x_p, w)
        outs[0].append(k_p.reshape(bsz, seq, heads, A_DV))
        outs[1].append(v_p.reshape(bsz, seq, heads, A_DV))
        outs[2].append(s_p.reshape(bsz, B_HEADS, dk, dv))

        ts = dbsz * pad_seq
        tms = _largest_divisor(ts, (256, 128))
        qts, _, _, k_s, v_s = attn_proj(x_s, tm=tms)
        zs, las = mix_proj(x_s, tm=tms)
        zs3 = zs.reshape(dbsz, pad_seq, zs.shape[1])
        q_s = qts.T.reshape(dbsz, pad_seq, d)
        k_s = k_s.reshape(dbsz, pad_seq * heads, A_DV)
        v_s = v_s.reshape(dbsz, pad_seq * heads, A_DV)
        n_group = _largest_divisor(n_pages, (16, 8, 4, 2, 1))
        oas = _attn_decode(page_table, q_s, k_s, v_s, cache_k[l].reshape(n_pool, page * heads, A_DV),
                           cache_v[l].reshape(n_pool, page * heads, A_DV), *lams, w["g_attn_row"],
                           n_group=n_group, lam_init=lam_init)
        obs, s_s = _gla_sample(zs3, las.reshape(dbsz, pad_seq, las.shape[1]), state_gla[l], d=d, n_valid=dseq)
        x_s = _mix_out(x_s, oas.reshape(ts, d), obs.reshape(ts, d), zs, w["g_gla_sub"], w["w_out"],
                       w["g_mix_post"], tm=_largest_divisor(ts, (512, 256, 128, 64, 32, 16, 8)))
        x_s = _ffn_block(x_s, w)
        outs[3].append(k_s.reshape(dbsz, pad_seq, heads, A_DV)[:, :dseq])
        outs[4].append(v_s.reshape(dbsz, pad_seq, heads, A_DV)[:, :dseq])
        outs[5].append(s_s)

    y_p = x_p.reshape(bsz, seq, d)
    y_s = x_s.reshape(dbsz, pad_seq, d)[:, :dseq]
    k_p, v_p, g_p, k_s, v_s, g_s = (jnp.stack(o) for o in outs)
    return (y_p, y_s, k_p, v_p, g_p, k_s, v_s, g_s)
```

```python
import functools
import math

import jax
import jax.numpy as jnp
from jax import lax
from jax.experimental import pallas as pl
from jax.experimental.pallas import tpu as pltpu

A_DQK = 64
A_DV = 2 * A_DQK
B_HEADS = 4
GATE_TAU = 16.0
EPS = 1e-6

LANES = 128
SUBLANES = 8
BF16_SUBLANES = 16
VMEM_LIMIT_BYTES = 56 * 1024 * 1024

NEG_BIG = -1e30
LOG2_E = math.log2(math.e)
ATTN_SCORE_BUFFERS = 6
ATTN_SCORE_LOOKAHEAD = 1
F32 = jnp.float32
BF16 = jnp.bfloat16


def _params(*semantics):
    return pltpu.CompilerParams(dimension_semantics=semantics, vmem_limit_bytes=VMEM_LIMIT_BYTES)


def _rms(x, g):
    return x * lax.rsqrt(jnp.mean(x * x, axis=-1, keepdims=True) + EPS) * g


def _lambda(lq1, lk1, lq2, lk2, lam_init):
    s1 = jnp.sum(lq1 * lk1, axis=-1, keepdims=True)
    s2 = jnp.sum(lq2 * lk2, axis=-1, keepdims=True)
    return jnp.exp(s1) - jnp.exp(s2) + lam_init


def _attn_proj_kernel(x_ref, g_ref, wq_t_ref, wv_t_ref, wk_ref, wv_ref,
                      qt_ref, vt_ref, kb_ref, ko_ref, vo_ref, *, q_scale, heads):
    tm = x_ref.shape[0]
    h = _rms(x_ref[...], g_ref[...]).astype(BF16)
    nt = (((1,), (1,)), ((), ()))
    qt = lax.dot_general(wq_t_ref[...], h, nt, preferred_element_type=F32)
    qt_ref[...] = (qt * q_scale).astype(BF16)
    vt = lax.dot_general(wv_t_ref[...], h, nt, preferred_element_type=F32)
    vt_ref[...] = vt.astype(BF16)
    k = jnp.dot(h, wk_ref[...], preferred_element_type=F32)
    kb_ref[...] = k.astype(BF16)
    v = jnp.dot(h, wv_ref[...], preferred_element_type=F32)
    for hd in range(heads):
        ko_ref[pl.ds(hd, tm, stride=heads), :] = k[:, hd * A_DV:(hd + 1) * A_DV]
        vo_ref[pl.ds(hd, tm, stride=heads), :] = v[:, hd * A_DV:(hd + 1) * A_DV]


def _attn_proj(x, g, wq_t, wv_t, wk, wv, *, tm):
    t, d = x.shape
    heads = d // A_DV
    const = lambda i: (0, 0)
    return pl.pallas_call(
        functools.partial(_attn_proj_kernel, q_scale=A_DQK ** -0.5 * LOG2_E, heads=heads),
        grid=(t // tm,),
        in_specs=[
            pl.BlockSpec((tm, d), lambda i: (i, 0)),
            pl.BlockSpec((1, d), const),
            pl.BlockSpec((d, d), const),
            pl.BlockSpec((d, d), const),
            pl.BlockSpec((d, d), const),
            pl.BlockSpec((d, d), const),
        ],
        out_specs=[
            pl.BlockSpec((d, tm), lambda i: (0, i)),
            pl.BlockSpec((d, tm), lambda i: (0, i)),
            pl.BlockSpec((tm, d), lambda i: (i, 0)),
            pl.BlockSpec((tm * heads, A_DV), lambda i: (i, 0)),
            pl.BlockSpec((tm * heads, A_DV), lambda i: (i, 0)),
        ],
        out_shape=[
            jax.ShapeDtypeStruct((d, t), BF16),
            jax.ShapeDtypeStruct((d, t), BF16),
            jax.ShapeDtypeStruct((t, d), BF16),
            jax.ShapeDtypeStruct((t * heads, A_DV), F32),
            jax.ShapeDtypeStruct((t * heads, A_DV), F32),
        ],
        compiler_params=_params("parallel"),
        name="attn_proj",
    )(x, g, wq_t, wv_t, wk, wv)


def _mix_proj_kernel(x_ref, g_ref, wg_ref, wgk_ref, bgk_ref, wm_ref, z_ref, la_ref, h_ref):
    @pl.when(pl.program_id(1) == 0)
    def _():
        h = _rms(x_ref[...], g_ref[...]).astype(BF16)
        h_ref[...] = h
        g_lr = jnp.dot(h, wg_ref[...], preferred_element_type=F32)
        gk = jnp.dot(g_lr.astype(BF16), wgk_ref[...], preferred_element_type=F32) + bgk_ref[...]
        la_ref[...] = jax.nn.log_sigmoid(gk) / GATE_TAU

    z_ref[...] = jnp.dot(h_ref[...], wm_ref[...], preferred_element_type=F32)


def _mix_proj(x, g, wg, wgk, bgk, wm, *, tm):
    t, d = x.shape
    n_tiles = wm.shape[1] // d
    wk = wgk.shape[1]
    const = lambda i, j: (0, 0)
    return pl.pallas_call(
        _mix_proj_kernel,
        grid=(t // tm, n_tiles),
        in_specs=[
            pl.BlockSpec((tm, d), lambda i, j: (i, 0)),
            pl.BlockSpec((1, d), const),
            pl.BlockSpec((d, LANES), const),
            pl.BlockSpec((LANES, wk), const),
            pl.BlockSpec((1, wk), const),
            pl.BlockSpec((d, d), lambda i, j: (0, j)),
        ],
        out_specs=[
            pl.BlockSpec((tm, d), lambda i, j: (i, j)),
            pl.BlockSpec((tm, wk), lambda i, j: (i, 0)),
        ],
        out_shape=[
            jax.ShapeDtypeStruct((t, n_tiles * d), F32),
            jax.ShapeDtypeStruct((t, wk), F32),
        ],
        scratch_shapes=[pltpu.VMEM((tm, d), BF16)],
        compiler_params=_params("parallel", "arbitrary"),
        name="mix_proj",
    )(x, g, wg, wgk, bgk, wm)


def _attn_prompt_kernel(qt_ref, k_ref, vt_ref, lq1_ref, lk1_ref, lq2_ref, lk2_ref, gcol_ref,
                        o_ref, qs_ref, s_refs, c_refs, m_ref, l_ref, acc_ref, *, bq, bk, lam_init):
    n_buf = len(s_refs)
    ratio = bq // bk
    i = pl.program_id(1)
    q = qt_ref[...]
    row = lax.broadcasted_iota(jnp.int32, q.shape, 0)
    zero = jnp.zeros_like(q)
    qs_ref[0] = jnp.where(row < A_DQK, q, zero)
    qs_ref[1] = jnp.where(row >= A_DQK, q, zero)

    m_ref[...] = jnp.full(m_ref.shape, NEG_BIG, F32)
    l_ref[...] = jnp.zeros(l_ref.shape, F32)
    acc_ref[...] = jnp.zeros(acc_ref.shape, F32)

    def scores(kb, slot):
        start = pl.multiple_of(kb * bk, bk)
        k = k_ref[pl.ds(start, bk), :]
        for mi in range(2):
            s = jnp.dot(k, qs_ref[mi], preferred_element_type=F32)
            s_refs[slot][mi] = s
            c_refs[slot][mi] = jnp.max(s, axis=0, keepdims=True)

    def absorb(kb, slot, diag):
        start = pl.multiple_of(kb * bk, bk)
        vt = vt_ref[:, pl.ds(start, bk)]
        vt1 = jnp.concatenate([vt, jnp.ones((BF16_SUBLANES, bk), BF16)], axis=0)
        for mi in range(2):
            s = s_refs[slot][mi]
            if diag is not None:
                kpos = lax.broadcasted_iota(jnp.int32, s.shape, 0) + diag * bk
                qpos = lax.broadcasted_iota(jnp.int32, s.shape, 1)
                s = jnp.where(kpos <= qpos, s, NEG_BIG)
            m_old = m_ref[mi]
            cmax = c_refs[slot][mi] if diag is None else jnp.max(s, axis=0, keepdims=True)
            m_new = jnp.maximum(m_old, cmax)
            alpha = jnp.exp2(m_old - m_new)
            p = jnp.exp2(s - m_new).astype(BF16)
            upd = jnp.dot(vt1, p, preferred_element_type=F32)
            l_ref[mi] = alpha * l_ref[mi] + upd[A_DV:A_DV + 1]
            acc_ref[mi] = alpha * acc_ref[mi] + upd[:A_DV]
            m_ref[mi] = m_new

    ahead = ATTN_SCORE_LOOKAHEAD
    assert ahead <= ratio and ahead < n_buf
    for j in range(ahead):
        scores(j, j)

    n_plain = i * ratio
    n_full = n_plain // n_buf

    def full_round(r, carry):
        kb = r * n_buf
        for j in range(n_buf):
            scores(kb + j + ahead, (j + ahead) % n_buf)
            absorb(kb + j, j, None)
        return carry

    lax.fori_loop(0, n_full, full_round, 0)

    kb0 = n_full * n_buf
    rest = n_plain + ratio - kb0
    for count in range(ratio, ratio + n_buf):
        def tail(count=count):
            for j in range(count):
                if j + ahead < count:
                    scores(kb0 + j + ahead, (j + ahead) % n_buf)
                absorb(kb0 + j, j % n_buf, None if j < count - ratio else j - (count - ratio))
        pl.when(rest == count)(tail)

    lam = _lambda(lq1_ref[...], lk1_ref[...], lq2_ref[...], lk2_ref[...], lam_init)
    o = acc_ref[0] * (1.0 / l_ref[0]) - acc_ref[1] * (lam / l_ref[1])
    ms = jnp.mean(o * o, axis=0, keepdims=True)
    y = o * lax.rsqrt(ms + EPS) * gcol_ref[...] * (1.0 - lam_init)
    o_ref[...] = y.T


def _attn_prompt(qt, kb, vt, lq1, lk1, lq2, lk2, gcol, *, bq, bk, lam_init):
    d, t = qt.shape
    heads = d // A_DV
    vec = pl.BlockSpec((1, A_DQK), lambda h, i: (0, 0))
    return pl.pallas_call(
        functools.partial(_attn_prompt_kernel, bq=bq, bk=bk, lam_init=lam_init),
        grid=(heads, t // bq),
        in_specs=[
            pl.BlockSpec((A_DV, bq), lambda h, i: (h, i)),
            pl.BlockSpec((t, A_DV), lambda h, i: (0, h)),
            pl.BlockSpec((A_DV, t), lambda h, i: (h, 0)),
            vec, vec, vec, vec,
            pl.BlockSpec((A_DV, 1), lambda h, i: (0, 0)),
        ],
        out_specs=pl.BlockSpec((bq, A_DV), lambda h, i: (i, h)),
        out_shape=jax.ShapeDtypeStruct((t, d), F32),
        scratch_shapes=[
            pltpu.VMEM((2, A_DV, bq), BF16),
            tuple(pltpu.VMEM((2, bk, bq), F32) for _ in range(ATTN_SCORE_BUFFERS)),
            tuple(pltpu.VMEM((2, 1, bq), F32) for _ in range(ATTN_SCORE_BUFFERS)),
            pltpu.VMEM((2, 1, bq), F32),
            pltpu.VMEM((2, 1, bq), F32),
            pltpu.VMEM((2, A_DV, bq), F32),
        ],
        compiler_params=_params("parallel", "arbitrary"),
        name="attn_prompt",
    )(qt, kb, vt, lq1, lk1, lq2, lk2, gcol)


def _attn_decode_kernel(pt_ref, q_ref, kn_ref, vn_ref, lq1_ref, lk1_ref, lq2_ref, lk2_ref, g_ref,
                        *rest, n_group, n_tok, lam_init):
    k_refs = rest[:n_group]
    v_refs = rest[n_group:2 * n_group]
    o_ref, qbd_ref, m_ref, l_ref, acc_ref, kpage_ref, vpage_ref, kcat_ref, vcat_ref = rest[2 * n_group:]
    j = pl.program_id(1)
    rows, d = qbd_ref.shape
    page = kpage_ref.shape[0]
    heads = d // A_DV

    def update(k, v, mask):
        nt = (((1,), (1,)), ((), ()))
        s = lax.dot_general(qbd_ref[...], k, nt, preferred_element_type=F32)
        if mask is not None:
            s = jnp.where(mask, s, NEG_BIG)
        m_old = m_ref[...]
        m_new = jnp.maximum(m_old, jnp.max(s, axis=1, keepdims=True))
        alpha = jnp.exp2(m_old - m_new)
        p = jnp.exp2(s - m_new)
        l_ref[...] = alpha * l_ref[...] + jnp.sum(p, axis=1, keepdims=True)
        acc_ref[...] = alpha * acc_ref[...] + jnp.dot(p.astype(BF16), v, preferred_element_type=F32)
        m_ref[...] = m_new

    def gather_heads(ref, n_keys):
        cols = [ref[0, pl.ds(h, n_keys, stride=heads), :] for h in range(heads)]
        return jnp.concatenate(cols, axis=1).astype(BF16)

    @pl.when(j == 0)
    def _():
        q = q_ref[0]
        qrep = jnp.concatenate([q] * (rows // n_tok), axis=0)
        r = lax.broadcasted_iota(jnp.int32, (rows, d), 0)
        c = lax.broadcasted_iota(jnp.int32, (rows, d), 1)
        qbd_ref[...] = jnp.where(r // n_tok == c // A_DQK, qrep, jnp.zeros_like(qrep))
        m_ref[...] = jnp.full(m_ref.shape, NEG_BIG, F32)
        l_ref[...] = jnp.zeros(l_ref.shape, F32)
        acc_ref[...] = jnp.zeros(acc_ref.shape, F32)
        kpage_ref[...] = jnp.zeros(kpage_ref.shape, BF16)
        vpage_ref[...] = jnp.zeros(vpage_ref.shape, BF16)
        kpage_ref[0:n_tok, :] = gather_heads(kn_ref, n_tok)
        vpage_ref[0:n_tok, :] = gather_heads(vn_ref, n_tok)
        tq = lax.broadcasted_iota(jnp.int32, (rows, page), 0) % n_tok
        ts = lax.broadcasted_iota(jnp.int32, (rows, page), 1)
        update(kpage_ref[...], vpage_ref[...], ts <= tq)

    def load_pages(refs, dst_ref):
        for g_idx, ref in enumerate(refs):
            dst_ref[g_idx * page:(g_idx + 1) * page, :] = gather_heads(ref, page)
        return dst_ref[...]

    update(load_pages(k_refs, kcat_ref), load_pages(v_refs, vcat_ref), None)

    @pl.when(j == pl.num_programs(1) - 1)
    def _():
        lam = _lambda(lq1_ref[...], lk1_ref[...], lq2_ref[...], lk2_ref[...], lam_init)
        on = acc_ref[...] / l_ref[...]
        g = g_ref[...]
        for h in range(heads):
            r0 = h * 2 * n_tok
            c0 = h * A_DV
            o = on[r0:r0 + n_tok, c0:c0 + A_DV] - lam * on[r0 + n_tok:r0 + 2 * n_tok, c0:c0 + A_DV]
            o_ref[0, :, c0:c0 + A_DV] = _rms(o, g) * (1.0 - lam_init)


def _attn_decode(page_table, q, k_new, v_new, cache_k, cache_v, lq1, lk1, lq2, lk2, g, *, n_group, lam_init):
    bsz, n_tok, d = q.shape
    n_pages = page_table.shape[1]
    heads = d // A_DV
    page = cache_k.shape[1] // heads
    rows = (d // A_DQK) * n_tok
    vec = pl.BlockSpec((1, A_DQK), lambda b, j, pt: (0, 0))

    def page_spec(g_idx):
        return pl.BlockSpec((1, page * heads, A_DV), lambda b, j, pt: (pt[b, j * n_group + g_idx], 0, 0))

    grid_spec = pltpu.PrefetchScalarGridSpec(
        num_scalar_prefetch=1,
        grid=(bsz, n_pages // n_group),
        in_specs=[
            pl.BlockSpec((1, n_tok, d), lambda b, j, pt: (b, 0, 0)),
            pl.BlockSpec((1, n_tok * heads, A_DV), lambda b, j, pt: (b, 0, 0)),
            pl.BlockSpec((1, n_tok * heads, A_DV), lambda b, j, pt: (b, 0, 0)),
            vec, vec, vec, vec,
            pl.BlockSpec((1, A_DV), lambda b, j, pt: (0, 0)),
        ] + [page_spec(g_idx) for g_idx in range(n_group)] * 2,
        out_specs=pl.BlockSpec((1, n_tok, d), lambda b, j, pt: (b, 0, 0)),
        scratch_shapes=[
            pltpu.VMEM((rows, d), BF16),
            pltpu.VMEM((rows, 1), F32),
            pltpu.VMEM((rows, 1), F32),
            pltpu.VMEM((rows, d), F32),
            pltpu.VMEM((page, d), BF16),
            pltpu.VMEM((page, d), BF16),
            pltpu.VMEM((n_group * page, d), BF16),
            pltpu.VMEM((n_group * page, d), BF16),
        ],
    )
    return pl.pallas_call(
        functools.partial(_attn_decode_kernel, n_group=n_group, n_tok=n_tok, lam_init=lam_init),
        grid_spec=grid_spec,
        out_shape=jax.ShapeDtypeStruct((bsz, n_tok, d), F32),
        compiler_params=_params("parallel", "arbitrary"),
        name="attn_decode",
    )(page_table, q, k_new, v_new, lq1, lk1, lq2, lk2, g, *([cache_k] * n_group), *([cache_v] * n_group))


def _cumsum_rows(la):
    c = la.shape[0]
    t_i = lax.broadcasted_iota(jnp.int32, (c, c), 0)
    s_i = lax.broadcasted_iota(jnp.int32, (c, c), 1)
    tri = (s_i <= t_i).astype(F32)
    return jnp.dot(tri, la, precision=lax.Precision.HIGHEST, preferred_element_type=F32)


def _gla_chunk(q, k, v, b, st, *, sub, scale):
    c, dk = q.shape
    q = q * scale
    b = b * LOG2_E
    nt = (((1,), (1,)), ((), ()))

    o = lax.dot_general((q * jnp.exp2(b)).astype(BF16), st.astype(BF16), nt, preferred_element_type=F32)

    lane = lax.broadcasted_iota(jnp.int32, (sub, c), 1)
    trow = lax.broadcasted_iota(jnp.int32, (sub, c), 0)
    a_rows = []
    for blk in range(c // sub):
        r0 = blk * sub
        qi = q[r0:r0 + sub]
        bi = b[r0:r0 + sub]
        a_blk = jnp.zeros((sub, c), F32)
        if blk > 0:
            ref = b[r0:r0 + 1]
            qs = qi * jnp.exp2(bi - ref)
            ks = k * jnp.exp2(jnp.minimum(ref - b, 0.0))
            a_off = lax.dot_general(qs.astype(BF16), ks.astype(BF16), nt, preferred_element_type=F32)
            a_blk = jnp.where(lane < r0, a_off, a_blk)
        for s in range(sub):
            ks_row = k[r0 + s:r0 + s + 1]
            bs_row = b[r0 + s:r0 + s + 1]
            x = qi * ks_row * jnp.exp2(bi - bs_row)
            col = jnp.sum(x, axis=1, keepdims=True)
            a_blk = jnp.where((lane == r0 + s) & (trow >= s), col, a_blk)
        a_rows.append(a_blk)
    a = a_rows[0] if len(a_rows) == 1 else jnp.concatenate(a_rows, axis=0)
    o = o + jnp.dot(a.astype(BF16), v.astype(BF16), preferred_element_type=F32)

    bl = b[c - 1:c]
    kd = k * jnp.exp2(bl - b)
    tn = (((0,), (0,)), ((), ()))
    st_new = st * jnp.exp2(bl) + lax.dot_general(v.astype(BF16), kd.astype(BF16), tn, preferred_element_type=F32)
    return o, st_new


def _gla_prompt_kernel(qk_ref, v_ref, la_ref, x_ref, oa_ref, r_ref, ga_ref, gb_ref, gsub_ref, wo_ref, gpost_ref,
                       y_ref, s_ref, st_ref, o_ref, *, chunk, sub, scale):
    t = pl.program_id(0)
    heads, dv, dk = st_ref.shape

    @pl.when(t == 0)
    def _():
        st_ref[...] = jnp.zeros(st_ref.shape, F32)

    def body(ci, carry):
        r0 = pl.multiple_of(ci * chunk, chunk)
        rows = pl.ds(r0, chunk)
        b_all = _cumsum_rows(la_ref[rows, :])
        for h in range(heads):
            o, st_new = _gla_chunk(qk_ref[rows, h * dk:(h + 1) * dk],
                                   qk_ref[rows, (heads + h) * dk:(heads + h + 1) * dk],
                                   v_ref[rows, h * dv:(h + 1) * dv],
                                   b_all[:, h * dk:(h + 1) * dk], st_ref[h], sub=sub, scale=scale)
            o_ref[rows, h * dv:(h + 1) * dv] = o
            st_ref[h] = st_new
        return carry

    lax.fori_loop(0, qk_ref.shape[0] // chunk, body, 0)

    @pl.when(t == pl.num_programs(0) - 1)
    def _():
        for h in range(heads):
            s_ref[h] = st_ref[h].T

    y_ref[...] = _mix(x_ref[...], oa_ref[...], o_ref[...], r_ref[...], ga_ref[...], gb_ref[...],
                      gsub_ref[...], wo_ref[...], gpost_ref[...])


def _gla_prompt(x, oa, z, la, gsub, wo, gpost, *, tc, chunk, sub):
    t, d = x.shape
    dk = d // (2 * B_HEADS)
    dv = d // B_HEADS
    const = lambda i: (0, 0)
    tile = lambda j: pl.BlockSpec((tc, d), lambda i: (i, j))
    return pl.pallas_call(
        functools.partial(_gla_prompt_kernel, chunk=chunk, sub=sub, scale=dk ** -0.5),
        grid=(t // tc,),
        in_specs=[
            tile(0), tile(1),
            pl.BlockSpec((tc, B_HEADS * dk), lambda i: (i, 0)),
            tile(0), tile(0), tile(2), tile(3), tile(4),
            pl.BlockSpec((1, gsub.shape[1]), const),
            pl.BlockSpec((d, d), const),
            pl.BlockSpec((1, d), const),
        ],
        out_specs=[
            pl.BlockSpec((tc, d), lambda i: (i, 0)),
            pl.BlockSpec((B_HEADS, dk, dv), lambda i: (0, 0, 0)),
        ],
        out_shape=[
            jax.ShapeDtypeStruct((t, d), F32),
            jax.ShapeDtypeStruct((B_HEADS, dk, dv), F32),
        ],
        scratch_shapes=[pltpu.VMEM((B_HEADS, dv, dk), F32), pltpu.VMEM((tc, d), F32)],
        compiler_params=_params("arbitrary"),
        name="gla_prompt",
    )(z, z, la, x, oa, z, z, z, gsub, wo, gpost)


def _gla_sample_kernel(qk_ref, v_ref, la_ref, s0_ref, o_ref, s_ref, *, n_valid, scale):
    c = qk_ref.shape[1]
    heads, dk, dv = s0_ref.shape[1:]
    valid = lax.broadcasted_iota(jnp.int32, (c, 1), 0) < n_valid
    b_all = _cumsum_rows(jnp.where(valid, la_ref[0], 0.0))
    for h in range(heads):
        k = jnp.where(valid, qk_ref[0, :, (heads + h) * dk:(heads + h + 1) * dk], 0.0)
        o, st_new = _gla_chunk(qk_ref[0, :, h * dk:(h + 1) * dk], k, v_ref[0, :, h * dv:(h + 1) * dv],
                               b_all[:, h * dk:(h + 1) * dk], s0_ref[0, h].T, sub=c, scale=scale)
        o_ref[0, :, h * dv:(h + 1) * dv] = o
        s_ref[0, h] = st_new.T


def _gla_sample(z3, la3, s0, *, d, n_valid):
    bsz, c, _ = z3.shape
    state_spec = pl.BlockSpec((1,) + s0.shape[1:], lambda b: (b, 0, 0, 0))
    return pl.pallas_call(
        functools.partial(_gla_sample_kernel, n_valid=n_valid, scale=s0.shape[2] ** -0.5),
        grid=(bsz,),
        in_specs=[
            pl.BlockSpec((1, c, d), lambda b: (b, 0, 0)),
            pl.BlockSpec((1, c, d), lambda b: (b, 0, 1)),
            pl.BlockSpec((1, c, la3.shape[2]), lambda b: (b, 0, 0)),
            state_spec,
        ],
        out_specs=[pl.BlockSpec((1, c, d), lambda b: (b, 0, 0)), state_spec],
        out_shape=[
            jax.ShapeDtypeStruct((bsz, c, d), F32),
            jax.ShapeDtypeStruct(s0.shape, F32),
        ],
        compiler_params=_params("parallel"),
        name="gla_sample",
    )(z3, z3, la3, s0)


def _mix(x, oa, ob, r, ga, gb, gsub, wo, gpost):
    dv = gsub.shape[1]
    ob_n = jnp.concatenate(
        [_rms(ob[:, h * dv:(h + 1) * dv], gsub) for h in range(ob.shape[1] // dv)], axis=1)
    ob_n = ob_n * jax.nn.silu(r)
    merged = jax.nn.sigmoid(ga) * oa + jax.nn.sigmoid(gb) * ob_n
    y = jnp.dot(merged.astype(BF16), wo, preferred_element_type=F32)
    return x + _rms(y, gpost)


def _mix_out_kernel(x_ref, oa_ref, ob_ref, r_ref, ga_ref, gb_ref, gsub_ref, wo_ref, gpost_ref, y_ref):
    y_ref[...] = _mix(x_ref[...], oa_ref[...], ob_ref[...], r_ref[...], ga_ref[...], gb_ref[...],
                      gsub_ref[...], wo_ref[...], gpost_ref[...])


def _mix_out(x, oa, ob, z, gsub, wo, gpost, *, tm):
    t, d = x.shape
    row = lambda i: (i, 0)
    const = lambda i: (0, 0)
    return pl.pallas_call(
        _mix_out_kernel,
        grid=(t // tm,),
        in_specs=[
            pl.BlockSpec((tm, d), row),
            pl.BlockSpec((tm, d), row),
            pl.BlockSpec((tm, d), row),
            pl.BlockSpec((tm, d), lambda i: (i, 2)),
            pl.BlockSpec((tm, d), lambda i: (i, 3)),
            pl.BlockSpec((tm, d), lambda i: (i, 4)),
            pl.BlockSpec((1, gsub.shape[1]), const),
            pl.BlockSpec((d, d), const),
            pl.BlockSpec((1, d), const),
        ],
        out_specs=pl.BlockSpec((tm, d), row),
        out_shape=jax.ShapeDtypeStruct((t, d), F32),
        compiler_params=_params("parallel"),
        name="mix_out",
    )(x, oa, ob, z, z, z, gsub, wo, gpost)


def _ffn_kernel(x_ref, gpre_ref, wg_ref, wu_ref, wd_ref, gpost_ref, y_ref, h_ref, acc_ref):
    j = pl.program_id(1)

    @pl.when(j == 0)
    def _():
        h_ref[...] = _rms(x_ref[...], gpre_ref[...]).astype(BF16)
        acc_ref[...] = jnp.zeros(acc_ref.shape, F32)

    h = h_ref[...]
    gate = jnp.dot(h, wg_ref[...], preferred_element_type=F32)
    up = jnp.dot(h, wu_ref[...], preferred_element_type=F32)
    act = (jax.nn.silu(gate) * up).astype(BF16)
    acc_ref[...] += jnp.dot(act, wd_ref[...], preferred_element_type=F32)

    @pl.when(j == pl.num_programs(1) - 1)
    def _():
        y_ref[...] = x_ref[...] + _rms(acc_ref[...], gpost_ref[...])


def _ffn(x, gpre, wg, wu, wd, gpost, *, tm, tf):
    t, d = x.shape
    dff = wg.shape[1]
    row = lambda i, j: (i, 0)
    const = lambda i, j: (0, 0)
    return pl.pallas_call(
        _ffn_kernel,
        grid=(t // tm, dff // tf),
        in_specs=[
            pl.BlockSpec((tm, d), row),
            pl.BlockSpec((1, d), const),
            pl.BlockSpec((d, tf), lambda i, j: (0, j)),
            pl.BlockSpec((d, tf), lambda i, j: (0, j)),
            pl.BlockSpec((tf, d), lambda i, j: (j, 0)),
            pl.BlockSpec((1, d), const),
        ],
        out_specs=pl.BlockSpec((tm, d), row),
        out_shape=jax.ShapeDtypeStruct((t, d), F32),
        scratch_shapes=[pltpu.VMEM((tm, d), BF16), pltpu.VMEM((tm, d), F32)],
        compiler_params=_params("parallel", "arbitrary"),
        name="ffn",
    )(x, gpre, wg, wu, wd, gpost)


def _largest_divisor(n, candidates):
    for c in candidates:
        if n % c == 0:
            return c
    raise ValueError(f"no supported tile for extent {n}")


def _prep_weights(g_mix_pre, w_in, w_gk, b_gk, g_attn_sub, g_gla_sub, w_out, g_mix_post,
                  g_ffn_pre, w_gate, w_up, w_down, g_ffn_post):
    d = w_in.shape[0]
    rank = w_gk.shape[0]
    a_w, wk, wv = d, d // 2, d
    off = [0]
    for s in (a_w, a_w, a_w, wk, wk, wv, rank, wv, d, d):
        off.append(off[-1] + s)
    col = lambda a, b: w_in[:, off[a]:off[b]]
    wq_t = col(0, 1).T.astype(BF16)
    wv_t = col(2, 3).T.astype(BF16)
    wm = jnp.concatenate([col(3, 5), col(5, 6), col(7, 8), col(8, 9), col(9, 10)], axis=1).astype(BF16)
    wg = jnp.pad(col(6, 7), ((0, 0), (0, LANES - rank))).astype(BF16)
    wgk = jnp.pad(w_gk, ((0, LANES - rank), (0, 0))).astype(BF16)
    row = lambda v: v.reshape(1, -1).astype(F32)
    return dict(
        g_mix_pre=row(g_mix_pre), wq_t=wq_t, wv_t=wv_t, wk=col(1, 2).astype(BF16), wv=col(2, 3).astype(BF16),
        wm=wm, wg=wg, wgk=wgk, bgk=row(b_gk),
        g_attn_row=row(g_attn_sub), g_attn_col=g_attn_sub.reshape(-1, 1).astype(F32),
        g_gla_sub=row(g_gla_sub), w_out=w_out.astype(BF16), g_mix_post=row(g_mix_post),
        g_ffn_pre=row(g_ffn_pre), w_gate=w_gate.astype(BF16), w_up=w_up.astype(BF16),
        w_down=w_down.astype(BF16), g_ffn_post=row(g_ffn_post))


def _ffn_block(x1, w):
    t = x1.shape[0]
    dff = w["w_gate"].shape[1]
    tf = dff // 2 if (dff // 2) % LANES == 0 else dff
    tmf = _largest_divisor(t, (1024, 512, 256, 128, 64, 32, 16, 8))
    return _ffn(x1, w["g_ffn_pre"], w["w_gate"], w["w_up"], w["w_down"], w["g_ffn_post"], tm=tmf, tf=tf)


def kernel(x_prompt, x_sample, cache_k, cache_v, state_gla, page_table, g_mix_pre, w_in, w_gk, b_gk,
           lambda_q1, lambda_k1, lambda_q2, lambda_k2, g_attn_sub, g_gla_sub, w_out, g_mix_post,
           g_ffn_pre, w_gate, w_up, w_down, g_ffn_post):
    depth = w_in.shape[0]
    bsz, seq, d = x_prompt.shape
    dbsz, dseq, _ = x_sample.shape
    heads = d // A_DV
    page = cache_k.shape[2]
    n_pool = cache_k.shape[1]
    n_pages = page_table.shape[1]
    dk = d // (2 * B_HEADS)
    dv = d // B_HEADS
    assert bsz == 1, "prompt kernels take one sequence"
    pad_seq = -(-dseq // SUBLANES) * SUBLANES

    x_p = x_prompt.reshape(seq, d)
    x_s = jnp.pad(x_sample, ((0, 0), (0, pad_seq - dseq), (0, 0))).reshape(dbsz * pad_seq, d)

    outs = ([], [], [], [], [], [])
    for l in range(depth):
        lam_init = 0.8 - 0.6 * math.exp(-0.3 * l)
        w = _prep_weights(g_mix_pre[l], w_in[l], w_gk[l], b_gk[l], g_attn_sub[l], g_gla_sub[l], w_out[l],
                          g_mix_post[l], g_ffn_pre[l], w_gate[l], w_up[l], w_down[l], g_ffn_post[l])
        lams = [v[l].reshape(1, A_DQK).astype(F32) for v in (lambda_q1, lambda_k1, lambda_q2, lambda_k2)]
        attn_proj = functools.partial(_attn_proj, g=w["g_mix_pre"], wq_t=w["wq_t"], wv_t=w["wv_t"],
                                      wk=w["wk"], wv=w["wv"])
        mix_proj = functools.partial(_mix_proj, g=w["g_mix_pre"], wg=w["wg"], wgk=w["wgk"], bgk=w["bgk"],
                                     wm=w["wm"])

        qt, vt, kb, k_p, v_p = attn_proj(x_p, tm=_largest_divisor(seq, (512, 256, 128)))
        z, la = mix_proj(x_p, tm=_largest_divisor(seq, (1024, 512, 256, 128)))
        blk = _largest_divisor(seq, (512, 256, 128))
        oa = _attn_prompt(qt, kb, vt, *lams, w["g_attn_col"], bq=blk, bk=blk, lam_init=lam_init)
        chunk = _largest_divisor(seq, (128, 64, 32, 16, 8))
        tc = _largest_divisor(seq, (512, 256, 128, 64, 32, 16, 8))
        x_p, s_p = _gla_prompt(x_p, oa, z, la, w["g_gla_sub"], w["w_out"], w["g_mix_post"],
                               tc=tc, chunk=chunk, sub=min(16, chunk))
        x_p = _ffn_block(x_p, w)
        outs[0].append(k_p.reshape(bsz, seq, heads, A_DV))
        outs[1].append(v_p.reshape(bsz, seq, heads, A_DV))
        outs[2].append(s_p.reshape(bsz, B_HEADS, dk, dv))

        ts = dbsz * pad_seq
        tms = _largest_divisor(ts, (256, 128))
        qts, _, _, k_s, v_s = attn_proj(x_s, tm=tms)
        zs, las = mix_proj(x_s, tm=tms)
        zs3 = zs.reshape(dbsz, pad_seq, zs.shape[1])
        q_s = qts.T.reshape(dbsz, pad_seq, d)
        k_s = k_s.reshape(dbsz, pad_seq * heads, A_DV)
        v_s = v_s.reshape(dbsz, pad_seq * heads, A_DV)
        n_group = _largest_divisor(n_pages, (16, 8, 4, 2, 1))
        oas = _attn_decode(page_table, q_s, k_s, v_s, cache_k[l].reshape(n_pool, page * heads, A_DV),
                           cache_v[l].reshape(n_pool, page * heads, A_DV), *lams, w["g_attn_row"],
                           n_group=n_group, lam_init=lam_init)
        obs, s_s = _gla_sample(zs3, las.reshape(dbsz, pad_seq, las.shape[1]), state_gla[l], d=d, n_valid=dseq)
        x_s = _mix_out(x_s, oas.reshape(ts, d), obs.reshape(ts, d), zs, w["g_gla_sub"], w["w_out"],
                       w["g_mix_post"], tm=_largest_divisor(ts, (512, 256, 128, 64, 32, 16, 8)))
        x_s = _ffn_block(x_s, w)
        outs[3].append(k_s.reshape(dbsz, pad_seq, heads, A_DV)[:, :dseq])
        outs[4].append(v_s.reshape(dbsz, pad_seq, heads, A_DV)[:, :dseq])
        outs[5].append(s_s)

    y_p = x_p.reshape(bsz, seq, d)
    y_s = x_s.reshape(dbsz, pad_seq, d)[:, :dseq]
    k_p, v_p, g_p, k_s, v_s, g_s = (jnp.stack(o) for o in outs)
    return (y_p, y_s, k_p, v_p, g_p, k_s, v_s, g_s)
```

```python
import functools
import math

import jax
import jax.numpy as jnp
from jax import lax
from jax.experimental import pallas as pl
from jax.experimental.pallas import tpu as pltpu

A_DQK = 64
A_DV = 2 * A_DQK
B_HEADS = 4
GATE_TAU = 16.0
EPS = 1e-6

LANES = 128
SUBLANES = 8
BF16_SUBLANES = 16
VMEM_LIMIT_BYTES = 56 * 1024 * 1024

NEG_BIG = -1e30
LOG2_E = math.log2(math.e)
ATTN_SCORE_BUFFERS = 6
ATTN_SCORE_LOOKAHEAD = 1
F32 = jnp.float32
BF16 = jnp.bfloat16


def _params(*semantics):
    return pltpu.CompilerParams(dimension_semantics=semantics, vmem_limit_bytes=VMEM_LIMIT_BYTES)


def _rms(x, g):
    return x * lax.rsqrt(jnp.mean(x * x, axis=-1, keepdims=True) + EPS) * g


def _lambda(lq1, lk1, lq2, lk2, lam_init):
    s1 = jnp.sum(lq1 * lk1, axis=-1, keepdims=True)
    s2 = jnp.sum(lq2 * lk2, axis=-1, keepdims=True)
    return jnp.exp(s1) - jnp.exp(s2) + lam_init


def _attn_proj_kernel(x_ref, g_ref, wq_t_ref, wv_t_ref, wk_ref, wv_ref,
                      qt_ref, vt_ref, kb_ref, ko_ref, vo_ref, *, q_scale, heads):
    tm = x_ref.shape[0]
    h = _rms(x_ref[...], g_ref[...]).astype(BF16)
    nt = (((1,), (1,)), ((), ()))
    qt = lax.dot_general(wq_t_ref[...], h, nt, preferred_element_type=F32)
    qt_ref[...] = (qt * q_scale).astype(BF16)
    vt = lax.dot_general(wv_t_ref[...], h, nt, preferred_element_type=F32)
    vt_ref[...] = vt.astype(BF16)
    k = jnp.dot(h, wk_ref[...], preferred_element_type=F32)
    kb_ref[...] = k.astype(BF16)
    v = jnp.dot(h, wv_ref[...], preferred_element_type=F32)
    for hd in range(heads):
        ko_ref[pl.ds(hd, tm, stride=heads), :] = k[:, hd * A_DV:(hd + 1) * A_DV]
        vo_ref[pl.ds(hd, tm, stride=heads), :] = v[:, hd * A_DV:(hd + 1) * A_DV]


def _attn_proj(x, g, wq_t, wv_t, wk, wv, *, tm):
    t, d = x.shape
    heads = d // A_DV
    const = lambda i: (0, 0)
    return pl.pallas_call(
        functools.partial(_attn_proj_kernel, q_scale=A_DQK ** -0.5 * LOG2_E, heads=heads),
        grid=(t // tm,),
        in_specs=[
            pl.BlockSpec((tm, d), lambda i: (i, 0)),
            pl.BlockSpec((1, d), const),
            pl.BlockSpec((d, d), const),
            pl.BlockSpec((d, d), const),
            pl.BlockSpec((d, d), const),
            pl.BlockSpec((d, d), const),
        ],
        out_specs=[
            pl.BlockSpec((d, tm), lambda i: (0, i)),
            pl.BlockSpec((d, tm), lambda i: (0, i)),
            pl.BlockSpec((tm, d), lambda i: (i, 0)),
            pl.BlockSpec((tm * heads, A_DV), lambda i: (i, 0)),
            pl.BlockSpec((tm * heads, A_DV), lambda i: (i, 0)),
        ],
        out_shape=[
            jax.ShapeDtypeStruct((d, t), BF16),
            jax.ShapeDtypeStruct((d, t), BF16),
            jax.ShapeDtypeStruct((t, d), BF16),
            jax.ShapeDtypeStruct((t * heads, A_DV), F32),
            jax.ShapeDtypeStruct((t * heads, A_DV), F32),
        ],
        compiler_params=_params("parallel"),
        name="attn_proj",
    )(x, g, wq_t, wv_t, wk, wv)


def _mix_proj_kernel(x_ref, g_ref, wg_ref, wgk_ref, bgk_ref, wm_ref, z_ref, la_ref, h_ref):
    @pl.when(pl.program_id(1) == 0)
    def _():
        h = _rms(x_ref[...], g_ref[...]).astype(BF16)
        h_ref[...] = h
        g_lr = jnp.dot(h, wg_ref[...], preferred_element_type=F32)
        gk = jnp.dot(g_lr.astype(BF16), wgk_ref[...], preferred_element_type=F32) + bgk_ref[...]
        la_ref[...] = jax.nn.log_sigmoid(gk) / GATE_TAU

    z_ref[...] = jnp.dot(h_ref[...], wm_ref[...], preferred_element_type=F32)


def _mix_proj(x, g, wg, wgk, bgk, wm, *, tm):
    t, d = x.shape
    n_tiles = wm.shape[1] // d
    wk = wgk.shape[1]
    const = lambda i, j: (0, 0)
    return pl.pallas_call(
        _mix_proj_kernel,
        grid=(t // tm, n_tiles),
        in_specs=[
            pl.BlockSpec((tm, d), lambda i, j: (i, 0)),
            pl.BlockSpec((1, d), const),
            pl.BlockSpec((d, LANES), const),
            pl.BlockSpec((LANES, wk), const),
            pl.BlockSpec((1, wk), const),
            pl.BlockSpec((d, d), lambda i, j: (0, j)),
        ],
        out_specs=[
            pl.BlockSpec((tm, d), lambda i, j: (i, j)),
            pl.BlockSpec((tm, wk), lambda i, j: (i, 0)),
        ],
        out_shape=[
            jax.ShapeDtypeStruct((t, n_tiles * d), F32),
            jax.ShapeDtypeStruct((t, wk), F32),
        ],
        scratch_shapes=[pltpu.VMEM((tm, d), BF16)],
        compiler_params=_params("parallel", "arbitrary"),
        name="mix_proj",
    )(x, g, wg, wgk, bgk, wm)


def _attn_prompt_kernel(qt_ref, k_ref, vt_ref, lq1_ref, lk1_ref, lq2_ref, lk2_ref, gcol_ref,
                        o_ref, qs_ref, s_refs, c_refs, m_ref, l_ref, acc_ref, *, bq, bk, lam_init):
    n_buf = len(s_refs)
    ratio = bq // bk
    n_sub = qt_ref.shape[1] // bq

    def scores(kb, slot):
        start = pl.multiple_of(kb * bk, bk)
        k = k_ref[pl.ds(start, bk), :]
        for mi in range(2):
            s = jnp.dot(k, qs_ref[mi], preferred_element_type=F32)
            s_refs[slot][mi] = s
            c_refs[slot][mi] = jnp.max(s, axis=0, keepdims=True)

    def absorb(kb, slot, diag):
        start = pl.multiple_of(kb * bk, bk)
        vt = vt_ref[:, pl.ds(start, bk)]
        vt1 = jnp.concatenate([vt, jnp.ones((BF16_SUBLANES, bk), BF16)], axis=0)
        for mi in range(2):
            s = s_refs[slot][mi]
            if diag is not None:
                kpos = lax.broadcasted_iota(jnp.int32, s.shape, 0) + diag * bk
                qpos = lax.broadcasted_iota(jnp.int32, s.shape, 1)
                s = jnp.where(kpos <= qpos, s, NEG_BIG)
            m_old = m_ref[mi]
            cmax = c_refs[slot][mi] if diag is None else jnp.max(s, axis=0, keepdims=True)
            m_new = jnp.maximum(m_old, cmax)
            alpha = jnp.exp2(m_old - m_new)
            p = jnp.exp2(s - m_new).astype(BF16)
            upd = jnp.dot(vt1, p, preferred_element_type=F32)
            l_ref[mi] = alpha * l_ref[mi] + upd[A_DV:A_DV + 1]
            acc_ref[mi] = alpha * acc_ref[mi] + upd[:A_DV]
            m_ref[mi] = m_new

    ahead = ATTN_SCORE_LOOKAHEAD
    assert ahead <= ratio and ahead < n_buf

    def full_round(r, carry):
        kb = r * n_buf
        for j in range(n_buf):
            scores(kb + j + ahead, (j + ahead) % n_buf)
            absorb(kb + j, j, None)
        return carry

    def query_block(u, carry):
        i = pl.program_id(1) * n_sub + u
        cols = pl.ds(pl.multiple_of(u * bq, bq), bq)
        q = qt_ref[:, cols]
        row = lax.broadcasted_iota(jnp.int32, q.shape, 0)
        zero = jnp.zeros_like(q)
        qs_ref[0] = jnp.where(row < A_DQK, q, zero)
        qs_ref[1] = jnp.where(row >= A_DQK, q, zero)
        m_ref[...] = jnp.full(m_ref.shape, NEG_BIG, F32)
        l_ref[...] = jnp.zeros(l_ref.shape, F32)
        acc_ref[...] = jnp.zeros(acc_ref.shape, F32)

        for j in range(ahead):
            scores(j, j)

        n_plain = i * ratio
        n_full = n_plain // n_buf
        lax.fori_loop(0, n_full, full_round, 0)

        kb0 = n_full * n_buf
        rest = n_plain + ratio - kb0
        for count in range(ratio, ratio + n_buf):
            def tail(count=count):
                for j in range(count):
                    if j + ahead < count:
                        scores(kb0 + j + ahead, (j + ahead) % n_buf)
                    absorb(kb0 + j, j % n_buf, None if j < count - ratio else j - (count - ratio))
            pl.when(rest == count)(tail)

        lam = _lambda(lq1_ref[...], lk1_ref[...], lq2_ref[...], lk2_ref[...], lam_init)
        o = acc_ref[0] * (1.0 / l_ref[0]) - acc_ref[1] * (lam / l_ref[1])
        ms = jnp.mean(o * o, axis=0, keepdims=True)
        y = o * lax.rsqrt(ms + EPS) * gcol_ref[...] * (1.0 - lam_init)
        o_ref[cols, :] = y.T
        return carry

    lax.fori_loop(0, n_sub, query_block, 0)


def _attn_prompt(qt, kb, vt, lq1, lk1, lq2, lk2, gcol, *, bq, bk, lam_init):
    d, t = qt.shape
    heads = d // A_DV
    n_sub = _largest_divisor(t // bq, (4, 2, 1))
    vec = pl.BlockSpec((1, A_DQK), lambda h, i: (0, 0))
    return pl.pallas_call(
        functools.partial(_attn_prompt_kernel, bq=bq, bk=bk, lam_init=lam_init),
        grid=(heads, t // (bq * n_sub)),
        in_specs=[
            pl.BlockSpec((A_DV, bq * n_sub), lambda h, i: (h, i)),
            pl.BlockSpec((t, A_DV), lambda h, i: (0, h)),
            pl.BlockSpec((A_DV, t), lambda h, i: (h, 0)),
            vec, vec, vec, vec,
            pl.BlockSpec((A_DV, 1), lambda h, i: (0, 0)),
        ],
        out_specs=pl.BlockSpec((bq * n_sub, A_DV), lambda h, i: (i, h)),
        out_shape=jax.ShapeDtypeStruct((t, d), F32),
        scratch_shapes=[
            pltpu.VMEM((2, A_DV, bq), BF16),
            tuple(pltpu.VMEM((2, bk, bq), F32) for _ in range(ATTN_SCORE_BUFFERS)),
            tuple(pltpu.VMEM((2, 1, bq), F32) for _ in range(ATTN_SCORE_BUFFERS)),
            pltpu.VMEM((2, 1, bq), F32),
            pltpu.VMEM((2, 1, bq), F32),
            pltpu.VMEM((2, A_DV, bq), F32),
        ],
        compiler_params=_params("parallel", "arbitrary"),
        name="attn_prompt",
    )(qt, kb, vt, lq1, lk1, lq2, lk2, gcol)


def _attn_decode_kernel(pt_ref, q_ref, kn_ref, vn_ref, lq1_ref, lk1_ref, lq2_ref, lk2_ref, g_ref,
                        *rest, n_group, n_tok, lam_init):
    k_refs = rest[:n_group]
    v_refs = rest[n_group:2 * n_group]
    o_ref, qbd_ref, m_ref, l_ref, acc_ref, kpage_ref, vpage_ref, kcat_ref, vcat_ref = rest[2 * n_group:]
    j = pl.program_id(1)
    rows, d = qbd_ref.shape
    page = kpage_ref.shape[0]
    heads = d // A_DV

    def update(k, v, mask):
        nt = (((1,), (1,)), ((), ()))
        s = lax.dot_general(qbd_ref[...], k, nt, preferred_element_type=F32)
        if mask is not None:
            s = jnp.where(mask, s, NEG_BIG)
        m_old = m_ref[...]
        m_new = jnp.maximum(m_old, jnp.max(s, axis=1, keepdims=True))
        alpha = jnp.exp2(m_old - m_new)
        p = jnp.exp2(s - m_new)
        l_ref[...] = alpha * l_ref[...] + jnp.sum(p, axis=1, keepdims=True)
        acc_ref[...] = alpha * acc_ref[...] + jnp.dot(p.astype(BF16), v, preferred_element_type=F32)
        m_ref[...] = m_new

    def gather_heads(ref, n_keys):
        cols = [ref[0, pl.ds(h, n_keys, stride=heads), :] for h in range(heads)]
        return jnp.concatenate(cols, axis=1).astype(BF16)

    @pl.when(j == 0)
    def _():
        q = q_ref[0]
        qrep = jnp.concatenate([q] * (rows // n_tok), axis=0)
        r = lax.broadcasted_iota(jnp.int32, (rows, d), 0)
        c = lax.broadcasted_iota(jnp.int32, (rows, d), 1)
        qbd_ref[...] = jnp.where(r // n_tok == c // A_DQK, qrep, jnp.zeros_like(qrep))
        m_ref[...] = jnp.full(m_ref.shape, NEG_BIG, F32)
        l_ref[...] = jnp.zeros(l_ref.shape, F32)
        acc_ref[...] = jnp.zeros(acc_ref.shape, F32)
        kpage_ref[...] = jnp.zeros(kpage_ref.shape, BF16)
        vpage_ref[...] = jnp.zeros(vpage_ref.shape, BF16)
        kpage_ref[0:n_tok, :] = gather_heads(kn_ref, n_tok)
        vpage_ref[0:n_tok, :] = gather_heads(vn_ref, n_tok)
        tq = lax.broadcasted_iota(jnp.int32, (rows, page), 0) % n_tok
        ts = lax.broadcasted_iota(jnp.int32, (rows, page), 1)
        update(kpage_ref[...], vpage_ref[...], ts <= tq)

    def load_pages(refs, dst_ref):
        for g_idx, ref in enumerate(refs):
            dst_ref[g_idx * page:(g_idx + 1) * page, :] = gather_heads(ref, page)
        return dst_ref[...]

    update(load_pages(k_refs, kcat_ref), load_pages(v_refs, vcat_ref), None)

    @pl.when(j == pl.num_programs(1) - 1)
    def _():
        lam = _lambda(lq1_ref[...], lk1_ref[...], lq2_ref[...], lk2_ref[...], lam_init)
        on = acc_ref[...] / l_ref[...]
        g = g_ref[...]
        for h in range(heads):
            r0 = h * 2 * n_tok
            c0 = h * A_DV
            o = on[r0:r0 + n_tok, c0:c0 + A_DV] - lam * on[r0 + n_tok:r0 + 2 * n_tok, c0:c0 + A_DV]
            o_ref[0, :, c0:c0 + A_DV] = _rms(o, g) * (1.0 - lam_init)


def _attn_decode(page_table, q, k_new, v_new, cache_k, cache_v, lq1, lk1, lq2, lk2, g, *, n_group, lam_init):
    bsz, n_tok, d = q.shape
    n_pages = page_table.shape[1]
    heads = d // A_DV
    page = cache_k.shape[1] // heads
    rows = (d // A_DQK) * n_tok
    vec = pl.BlockSpec((1, A_DQK), lambda b, j, pt: (0, 0))

    def page_spec(g_idx):
        return pl.BlockSpec((1, page * heads, A_DV), lambda b, j, pt: (pt[b, j * n_group + g_idx], 0, 0))

    grid_spec = pltpu.PrefetchScalarGridSpec(
        num_scalar_prefetch=1,
        grid=(bsz, n_pages // n_group),
        in_specs=[
            pl.BlockSpec((1, n_tok, d), lambda b, j, pt: (b, 0, 0)),
            pl.BlockSpec((1, n_tok * heads, A_DV), lambda b, j, pt: (b, 0, 0)),
            pl.BlockSpec((1, n_tok * heads, A_DV), lambda b, j, pt: (b, 0, 0)),
            vec, vec, vec, vec,
            pl.BlockSpec((1, A_DV), lambda b, j, pt: (0, 0)),
        ] + [page_spec(g_idx) for g_idx in range(n_group)] * 2,
        out_specs=pl.BlockSpec((1, n_tok, d), lambda b, j, pt: (b, 0, 0)),
        scratch_shapes=[
            pltpu.VMEM((rows, d), BF16),
            pltpu.VMEM((rows, 1), F32),
            pltpu.VMEM((rows, 1), F32),
            pltpu.VMEM((rows, d), F32),
            pltpu.VMEM((page, d), BF16),
            pltpu.VMEM((page, d), BF16),
            pltpu.VMEM((n_group * page, d), BF16),
            pltpu.VMEM((n_group * page, d), BF16),
        ],
    )
    return pl.pallas_call(
        functools.partial(_attn_decode_kernel, n_group=n_group, n_tok=n_tok, lam_init=lam_init),
        grid_spec=grid_spec,
        out_shape=jax.ShapeDtypeStruct((bsz, n_tok, d), F32),
        compiler_params=_params("parallel", "arbitrary"),
        name="attn_decode",
    )(page_table, q, k_new, v_new, lq1, lk1, lq2, lk2, g, *([cache_k] * n_group), *([cache_v] * n_group))


def _cumsum_rows(la):
    c = la.shape[0]
    t_i = lax.broadcasted_iota(jnp.int32, (c, c), 0)
    s_i = lax.broadcasted_iota(jnp.int32, (c, c), 1)
    tri = (s_i <= t_i).astype(F32)
    return jnp.dot(tri, la, precision=lax.Precision.HIGHEST, preferred_element_type=F32)


def _gla_chunk(q, k, v, b, st, *, sub, scale):
    c, dk = q.shape
    q = q * scale
    b = b * LOG2_E
    nt = (((1,), (1,)), ((), ()))

    o = lax.dot_general((q * jnp.exp2(b)).astype(BF16), st.astype(BF16), nt, preferred_element_type=F32)

    lane = lax.broadcasted_iota(jnp.int32, (sub, c), 1)
    trow = lax.broadcasted_iota(jnp.int32, (sub, c), 0)
    a_rows = []
    for blk in range(c // sub):
        r0 = blk * sub
        qi = q[r0:r0 + sub]
        bi = b[r0:r0 + sub]
        a_blk = jnp.zeros((sub, c), F32)
        if blk > 0:
            ref = b[r0:r0 + 1]
            qs = qi * jnp.exp2(bi - ref)
            ks = k * jnp.exp2(jnp.minimum(ref - b, 0.0))
            a_off = lax.dot_general(qs.astype(BF16), ks.astype(BF16), nt, preferred_element_type=F32)
            a_blk = jnp.where(lane < r0, a_off, a_blk)
        for s in range(sub):
            ks_row = k[r0 + s:r0 + s + 1]
            bs_row = b[r0 + s:r0 + s + 1]
            x = qi * ks_row * jnp.exp2(bi - bs_row)
            col = jnp.sum(x, axis=1, keepdims=True)
            a_blk = jnp.where((lane == r0 + s) & (trow >= s), col, a_blk)
        a_rows.append(a_blk)
    a = a_rows[0] if len(a_rows) == 1 else jnp.concatenate(a_rows, axis=0)
    o = o + jnp.dot(a.astype(BF16), v.astype(BF16), preferred_element_type=F32)

    bl = b[c - 1:c]
    kd = k * jnp.exp2(bl - b)
    tn = (((0,), (0,)), ((), ()))
    st_new = st * jnp.exp2(bl) + lax.dot_general(v.astype(BF16), kd.astype(BF16), tn, preferred_element_type=F32)
    return o, st_new


def _gla_prompt_kernel(qk_ref, v_ref, la_ref, x_ref, oa_ref, r_ref, ga_ref, gb_ref, gsub_ref, wo_ref, gpost_ref,
                       y_ref, s_ref, st_ref, o_ref, *, chunk, sub, scale):
    t = pl.program_id(0)
    heads, dv, dk = st_ref.shape

    @pl.when(t == 0)
    def _():
        st_ref[...] = jnp.zeros(st_ref.shape, F32)

    def body(ci, carry):
        r0 = pl.multiple_of(ci * chunk, chunk)
        rows = pl.ds(r0, chunk)
        b_all = _cumsum_rows(la_ref[rows, :])
        for h in range(heads):
            o, st_new = _gla_chunk(qk_ref[rows, h * dk:(h + 1) * dk],
                                   qk_ref[rows, (heads + h) * dk:(heads + h + 1) * dk],
                                   v_ref[rows, h * dv:(h + 1) * dv],
                                   b_all[:, h * dk:(h + 1) * dk], st_ref[h], sub=sub, scale=scale)
            o_ref[rows, h * dv:(h + 1) * dv] = o
            st_ref[h] = st_new
        return carry

    lax.fori_loop(0, qk_ref.shape[0] // chunk, body, 0)

    @pl.when(t == pl.num_programs(0) - 1)
    def _():
        for h in range(heads):
            s_ref[h] = st_ref[h].T

    y_ref[...] = _mix(x_ref[...], oa_ref[...], o_ref[...], r_ref[...], ga_ref[...], gb_ref[...],
                      gsub_ref[...], wo_ref[...], gpost_ref[...])


def _gla_prompt(x, oa, z, la, gsub, wo, gpost, *, tc, chunk, sub):
    t, d = x.shape
    dk = d // (2 * B_HEADS)
    dv = d // B_HEADS
    const = lambda i: (0, 0)
    tile = lambda j: pl.BlockSpec((tc, d), lambda i: (i, j))
    return pl.pallas_call(
        functools.partial(_gla_prompt_kernel, chunk=chunk, sub=sub, scale=dk ** -0.5),
        grid=(t // tc,),
        in_specs=[
            tile(0), tile(1),
            pl.BlockSpec((tc, B_HEADS * dk), lambda i: (i, 0)),
            tile(0), tile(0), tile(2), tile(3), tile(4),
            pl.BlockSpec((1, gsub.shape[1]), const),
            pl.BlockSpec((d, d), const),
            pl.BlockSpec((1, d), const),
        ],
        out_specs=[
            pl.BlockSpec((tc, d), lambda i: (i, 0)),
            pl.BlockSpec((B_HEADS, dk, dv), lambda i: (0, 0, 0)),
        ],
        out_shape=[
            jax.ShapeDtypeStruct((t, d), F32),
            jax.ShapeDtypeStruct((B_HEADS, dk, dv), F32),
        ],
        scratch_shapes=[pltpu.VMEM((B_HEADS, dv, dk), F32), pltpu.VMEM((tc, d), F32)],
        compiler_params=_params("arbitrary"),
        name="gla_prompt",
    )(z, z, la, x, oa, z, z, z, gsub, wo, gpost)


def _gla_sample_kernel(qk_ref, v_ref, la_ref, s0_ref, o_ref, s_ref, *, n_valid, scale):
    c = qk_ref.shape[1]
    heads, dk, dv = s0_ref.shape[1:]
    valid = lax.broadcasted_iota(jnp.int32, (c, 1), 0) < n_valid
    b_all = _cumsum_rows(jnp.where(valid, la_ref[0], 0.0))
    for h in range(heads):
        k = jnp.where(valid, qk_ref[0, :, (heads + h) * dk:(heads + h + 1) * dk], 0.0)
        o, st_new = _gla_chunk(qk_ref[0, :, h * dk:(h + 1) * dk], k, v_ref[0, :, h * dv:(h + 1) * dv],
                               b_all[:, h * dk:(h + 1) * dk], s0_ref[0, h].T, sub=c, scale=scale)
        o_ref[0, :, h * dv:(h + 1) * dv] = o
        s_ref[0, h] = st_new.T


def _gla_sample(z3, la3, s0, *, d, n_valid):
    bsz, c, _ = z3.shape
    state_spec = pl.BlockSpec((1,) + s0.shape[1:], lambda b: (b, 0, 0, 0))
    return pl.pallas_call(
        functools.partial(_gla_sample_kernel, n_valid=n_valid, scale=s0.shape[2] ** -0.5),
        grid=(bsz,),
        in_specs=[
            pl.BlockSpec((1, c, d), lambda b: (b, 0, 0)),
            pl.BlockSpec((1, c, d), lambda b: (b, 0, 1)),
            pl.BlockSpec((1, c, la3.shape[2]), lambda b: (b, 0, 0)),
            state_spec,
        ],
        out_specs=[pl.BlockSpec((1, c, d), lambda b: (b, 0, 0)), state_spec],
        out_shape=[
            jax.ShapeDtypeStruct((bsz, c, d), F32),
            jax.ShapeDtypeStruct(s0.shape, F32),
        ],
        compiler_params=_params("parallel"),
        name="gla_sample",
    )(z3, z3, la3, s0)


def _mix(x, oa, ob, r, ga, gb, gsub, wo, gpost):
    dv = gsub.shape[1]
    ob_n = jnp.concatenate(
        [_rms(ob[:, h * dv:(h + 1) * dv], gsub) for h in range(ob.shape[1] // dv)], axis=1)
    ob_n = ob_n * jax.nn.silu(r)
    merged = jax.nn.sigmoid(ga) * oa + jax.nn.sigmoid(gb) * ob_n
    y = jnp.dot(merged.astype(BF16), wo, preferred_element_type=F32)
    return x + _rms(y, gpost)


def _mix_out_kernel(x_ref, oa_ref, ob_ref, r_ref, ga_ref, gb_ref, gsub_ref, wo_ref, gpost_ref, y_ref):
    y_ref[...] = _mix(x_ref[...], oa_ref[...], ob_ref[...], r_ref[...], ga_ref[...], gb_ref[...],
                      gsub_ref[...], wo_ref[...], gpost_ref[...])


def _mix_out(x, oa, ob, z, gsub, wo, gpost, *, tm):
    t, d = x.shape
    row = lambda i: (i, 0)
    const = lambda i: (0, 0)
    return pl.pallas_call(
        _mix_out_kernel,
        grid=(t // tm,),
        in_specs=[
            pl.BlockSpec((tm, d), row),
            pl.BlockSpec((tm, d), row),
            pl.BlockSpec((tm, d), row),
            pl.BlockSpec((tm, d), lambda i: (i, 2)),
            pl.BlockSpec((tm, d), lambda i: (i, 3)),
            pl.BlockSpec((tm, d), lambda i: (i, 4)),
            pl.BlockSpec((1, gsub.shape[1]), const),
            pl.BlockSpec((d, d), const),
            pl.BlockSpec((1, d), const),
        ],
        out_specs=pl.BlockSpec((tm, d), row),
        out_shape=jax.ShapeDtypeStruct((t, d), F32),
        compiler_params=_params("parallel"),
        name="mix_out",
    )(x, oa, ob, z, z, z, gsub, wo, gpost)


def _ffn_kernel(x_ref, gpre_ref, wg_ref, wu_ref, wd_ref, gpost_ref, y_ref, h_ref, acc_ref):
    j = pl.program_id(1)

    @pl.when(j == 0)
    def _():
        h_ref[...] = _rms(x_ref[...], gpre_ref[...]).astype(BF16)
        acc_ref[...] = jnp.zeros(acc_ref.shape, F32)

    h = h_ref[...]
    gate = jnp.dot(h, wg_ref[...], preferred_element_type=F32)
    up = jnp.dot(h, wu_ref[...], preferred_element_type=F32)
    act = (jax.nn.silu(gate) * up).astype(BF16)
    acc_ref[...] += jnp.dot(act, wd_ref[...], preferred_element_type=F32)

    @pl.when(j == pl.num_programs(1) - 1)
    def _():
        y_ref[...] = x_ref[...] + _rms(acc_ref[...], gpost_ref[...])


def _ffn(x, gpre, wg, wu, wd, gpost, *, tm, tf):
    t, d = x.shape
    dff = wg.shape[1]
    row = lambda i, j: (i, 0)
    const = lambda i, j: (0, 0)
    return pl.pallas_call(
        _ffn_kernel,
        grid=(t // tm, dff // tf),
        in_specs=[
            pl.BlockSpec((tm, d), row),
            pl.BlockSpec((1, d), const),
            pl.BlockSpec((d, tf), lambda i, j: (0, j)),
            pl.BlockSpec((d, tf), lambda i, j: (0, j)),
            pl.BlockSpec((tf, d), lambda i, j: (j, 0)),
            pl.BlockSpec((1, d), const),
        ],
        out_specs=pl.BlockSpec((tm, d), row),
        out_shape=jax.ShapeDtypeStruct((t, d), F32),
        scratch_shapes=[pltpu.VMEM((tm, d), BF16), pltpu.VMEM((tm, d), F32)],
        compiler_params=_params("parallel", "arbitrary"),
        name="ffn",
    )(x, gpre, wg, wu, wd, gpost)


def _largest_divisor(n, candidates):
    for c in candidates:
        if n % c == 0:
            return c
    raise ValueError(f"no supported tile for extent {n}")


def _prep_weights(g_mix_pre, w_in, w_gk, b_gk, g_attn_sub, g_gla_sub, w_out, g_mix_post,
                  g_ffn_pre, w_gate, w_up, w_down, g_ffn_post):
    d = w_in.shape[0]
    rank = w_gk.shape[0]
    a_w, wk, wv = d, d // 2, d
    off = [0]
    for s in (a_w, a_w, a_w, wk, wk, wv, rank, wv, d, d):
        off.append(off[-1] + s)
    col = lambda a, b: w_in[:, off[a]:off[b]]
    wq_t = col(0, 1).T.astype(BF16)
    wv_t = col(2, 3).T.astype(BF16)
    wm = jnp.concatenate([col(3, 5), col(5, 6), col(7, 8), col(8, 9), col(9, 10)], axis=1).astype(BF16)
    wg = jnp.pad(col(6, 7), ((0, 0), (0, LANES - rank))).astype(BF16)
    wgk = jnp.pad(w_gk, ((0, LANES - rank), (0, 0))).astype(BF16)
    row = lambda v: v.reshape(1, -1).astype(F32)
    return dict(
        g_mix_pre=row(g_mix_pre), wq_t=wq_t, wv_t=wv_t, wk=col(1, 2).astype(BF16), wv=col(2, 3).astype(BF16),
        wm=wm, wg=wg, wgk=wgk, bgk=row(b_gk),
        g_attn_row=row(g_attn_sub), g_attn_col=g_attn_sub.reshape(-1, 1).astype(F32),
        g_gla_sub=row(g_gla_sub), w_out=w_out.astype(BF16), g_mix_post=row(g_mix_post),
        g_ffn_pre=row(g_ffn_pre), w_gate=w_gate.astype(BF16), w_up=w_up.astype(BF16),
        w_down=w_down.astype(BF16), g_ffn_post=row(g_ffn_post))


def _ffn_block(x1, w):
    t = x1.shape[0]
    dff = w["w_gate"].shape[1]
    tf = dff // 2 if (dff // 2) % LANES == 0 else dff
    tmf = _largest_divisor(t, (1024, 512, 256, 128, 64, 32, 16, 8))
    return _ffn(x1, w["g_ffn_pre"], w["w_gate"], w["w_up"], w["w_down"], w["g_ffn_post"], tm=tmf, tf=tf)


def kernel(x_prompt, x_sample, cache_k, cache_v, state_gla, page_table, g_mix_pre, w_in, w_gk, b_gk,
           lambda_q1, lambda_k1, lambda_q2, lambda_k2, g_attn_sub, g_gla_sub, w_out, g_mix_post,
           g_ffn_pre, w_gate, w_up, w_down, g_ffn_post):
    depth = w_in.shape[0]
    bsz, seq, d = x_prompt.shape
    dbsz, dseq, _ = x_sample.shape
    heads = d // A_DV
    page = cache_k.shape[2]
    n_pool = cache_k.shape[1]
    n_pages = page_table.shape[1]
    dk = d // (2 * B_HEADS)
    dv = d // B_HEADS
    assert bsz == 1, "prompt kernels take one sequence"
    pad_seq = -(-dseq // SUBLANES) * SUBLANES

    x_p = x_prompt.reshape(seq, d)
    x_s = jnp.pad(x_sample, ((0, 0), (0, pad_seq - dseq), (0, 0))).reshape(dbsz * pad_seq, d)

    outs = ([], [], [], [], [], [])
    for l in range(depth):
        lam_init = 0.8 - 0.6 * math.exp(-0.3 * l)
        w = _prep_weights(g_mix_pre[l], w_in[l], w_gk[l], b_gk[l], g_attn_sub[l], g_gla_sub[l], w_out[l],
                          g_mix_post[l], g_ffn_pre[l], w_gate[l], w_up[l], w_down[l], g_ffn_post[l])
        lams = [v[l].reshape(1, A_DQK).astype(F32) for v in (lambda_q1, lambda_k1, lambda_q2, lambda_k2)]
        attn_proj = functools.partial(_attn_proj, g=w["g_mix_pre"], wq_t=w["wq_t"], wv_t=w["wv_t"],
                                      wk=w["wk"], wv=w["wv"])
        mix_proj = functools.partial(_mix_proj, g=w["g_mix_pre"], wg=w["wg"], wgk=w["wgk"], bgk=w["bgk"],
                                     wm=w["wm"])

        qt, vt, kb, k_p, v_p = attn_proj(x_p, tm=_largest_divisor(seq, (512, 256, 128)))
        z, la = mix_proj(x_p, tm=_largest_divisor(seq, (1024, 512, 256, 128)))
        blk = _largest_divisor(seq, (512, 256, 128))
        oa = _attn_prompt(qt, kb, vt, *lams, w["g_attn_col"], bq=blk, bk=blk, lam_init=lam_init)
        chunk = _largest_divisor(seq, (128, 64, 32, 16, 8))
        tc = _largest_divisor(seq, (512, 256, 128, 64, 32, 16, 8))
        x_p, s_p = _gla_prompt(x_p, oa, z, la, w["g_gla_sub"], w["w_out"], w["g_mix_post"],
                               tc=tc, chunk=chunk, sub=min(16, chunk))
        x_p = _ffn_block(x_p, w)
        outs[0].append(k_p.reshape(bsz, seq, heads, A_DV))
        outs[1].append(v_p.reshape(bsz, seq, heads, A_DV))
        outs[2].append(s_p.reshape(bsz, B_HEADS, dk, dv))

        ts = dbsz * pad_seq
        tms = _largest_divisor(ts, (256, 128))
        qts, _, _, k_s, v_s = attn_proj(x_s, tm=tms)
        zs, las = mix_proj(x_s, tm=tms)
        zs3 = zs.reshape(dbsz, pad_seq, zs.shape[1])
        q_s = qts.T.reshape(dbsz, pad_seq, d)
        k_s = k_s.reshape(dbsz, pad_seq * heads, A_DV)
        v_s = v_s.reshape(dbsz, pad_seq * heads, A_DV)
        n_group = _largest_divisor(n_pages, (16, 8, 4, 2, 1))
        oas = _attn_decode(page_table, q_s, k_s, v_s, cache_k[l].reshape(n_pool, page * heads, A_DV),
                           cache_v[l].reshape(n_pool, page * heads, A_DV), *lams, w["g_attn_row"],
                           n_group=n_group, lam_init=lam_init)
        obs, s_s = _gla_sample(zs3, las.reshape(dbsz, pad_seq, las.shape[1]), state_gla[l], d=d, n_valid=dseq)
        x_s = _mix_out(x_s, oas.reshape(ts, d), obs.reshape(ts, d), zs, w["g_gla_sub"], w["w_out"],
                       w["g_mix_post"], tm=_largest_divisor(ts, (512, 256, 128, 64, 32, 16, 8)))
        x_s = _ffn_block(x_s, w)
        outs[3].append(k_s.reshape(dbsz, pad_seq, heads, A_DV)[:, :dseq])
        outs[4].append(v_s.reshape(dbsz, pad_seq, heads, A_DV)[:, :dseq])
        outs[5].append(s_s)

    y_p = x_p.reshape(bsz, seq, d)
    y_s = x_s.reshape(dbsz, pad_seq, d)[:, :dseq]
    k_p, v_p, g_p, k_s, v_s, g_s = (jnp.stack(o) for o in outs)
    return (y_p, y_s, k_p, v_p, g_p, k_s, v_s, g_s)
```

```python
import functools
import math

import jax
import jax.numpy as jnp
from jax import lax
from jax.experimental import pallas as pl
from jax.experimental.pallas import tpu as pltpu

A_DQK = 64
A_DV = 2 * A_DQK
B_HEADS = 4
GATE_TAU = 16.0
EPS = 1e-6

LANES = 128
SUBLANES = 8
BF16_SUBLANES = 16
VMEM_LIMIT_BYTES = 56 * 1024 * 1024

NEG_BIG = -1e30
LOG2_E = math.log2(math.e)
ATTN_SCORE_BUFFERS = 8
ATTN_SCORE_LOOKAHEAD = 1
F32 = jnp.float32
BF16 = jnp.bfloat16


def _params(*semantics):
    return pltpu.CompilerParams(dimension_semantics=semantics, vmem_limit_bytes=VMEM_LIMIT_BYTES)


def _rms(x, g):
    return x * lax.rsqrt(jnp.mean(x * x, axis=-1, keepdims=True) + EPS) * g


def _lambda(lq1, lk1, lq2, lk2, lam_init):
    s1 = jnp.sum(lq1 * lk1, axis=-1, keepdims=True)
    s2 = jnp.sum(lq2 * lk2, axis=-1, keepdims=True)
    return jnp.exp(s1) - jnp.exp(s2) + lam_init


def _attn_proj_kernel(x_ref, g_ref, wq_t_ref, wv_t_ref, wk_ref, wv_ref,
                      qt_ref, vt_ref, kb_ref, ko_ref, vo_ref, *, q_scale, heads):
    tm = x_ref.shape[0]
    h = _rms(x_ref[...], g_ref[...]).astype(BF16)
    nt = (((1,), (1,)), ((), ()))
    qt = lax.dot_general(wq_t_ref[...], h, nt, preferred_element_type=F32)
    qt_ref[...] = (qt * q_scale).astype(BF16)
    vt = lax.dot_general(wv_t_ref[...], h, nt, preferred_element_type=F32)
    vt_ref[...] = vt.astype(BF16)
    k = jnp.dot(h, wk_ref[...], preferred_element_type=F32)
    kb_ref[...] = k.astype(BF16)
    v = jnp.dot(h, wv_ref[...], preferred_element_type=F32)
    for hd in range(heads):
        ko_ref[pl.ds(hd, tm, stride=heads), :] = k[:, hd * A_DV:(hd + 1) * A_DV]
        vo_ref[pl.ds(hd, tm, stride=heads), :] = v[:, hd * A_DV:(hd + 1) * A_DV]


def _attn_proj(x, g, wq_t, wv_t, wk, wv, *, tm):
    t, d = x.shape
    heads = d // A_DV
    const = lambda i: (0, 0)
    return pl.pallas_call(
        functools.partial(_attn_proj_kernel, q_scale=A_DQK ** -0.5 * LOG2_E, heads=heads),
        grid=(t // tm,),
        in_specs=[
            pl.BlockSpec((tm, d), lambda i: (i, 0)),
            pl.BlockSpec((1, d), const),
            pl.BlockSpec((d, d), const),
            pl.BlockSpec((d, d), const),
            pl.BlockSpec((d, d), const),
            pl.BlockSpec((d, d), const),
        ],
        out_specs=[
            pl.BlockSpec((d, tm), lambda i: (0, i)),
            pl.BlockSpec((d, tm), lambda i: (0, i)),
            pl.BlockSpec((tm, d), lambda i: (i, 0)),
            pl.BlockSpec((tm * heads, A_DV), lambda i: (i, 0)),
            pl.BlockSpec((tm * heads, A_DV), lambda i: (i, 0)),
        ],
        out_shape=[
            jax.ShapeDtypeStruct((d, t), BF16),
            jax.ShapeDtypeStruct((d, t), BF16),
            jax.ShapeDtypeStruct((t, d), BF16),
            jax.ShapeDtypeStruct((t * heads, A_DV), F32),
            jax.ShapeDtypeStruct((t * heads, A_DV), F32),
        ],
        compiler_params=_params("parallel"),
        name="attn_proj",
    )(x, g, wq_t, wv_t, wk, wv)


def _mix_proj_kernel(x_ref, g_ref, wg_ref, wgk_ref, bgk_ref, wm_ref, z_ref, la_ref):
    h = _rms(x_ref[...], g_ref[...]).astype(BF16)
    g_lr = jnp.dot(h, wg_ref[...], preferred_element_type=F32)
    gk = jnp.dot(g_lr.astype(BF16), wgk_ref[...], preferred_element_type=F32) + bgk_ref[...]
    la_ref[...] = jax.nn.log_sigmoid(gk) / GATE_TAU
    z_ref[...] = jnp.dot(h, wm_ref[...], preferred_element_type=F32)


def _mix_proj(x, g, wg, wgk, bgk, wm, *, tm):
    t, d = x.shape
    n = wm.shape[1]
    wk = wgk.shape[1]
    const = lambda i: (0, 0)
    return pl.pallas_call(
        _mix_proj_kernel,
        grid=(t // tm,),
        in_specs=[
            pl.BlockSpec((tm, d), lambda i: (i, 0)),
            pl.BlockSpec((1, d), const),
            pl.BlockSpec((d, LANES), const),
            pl.BlockSpec((LANES, wk), const),
            pl.BlockSpec((1, wk), const),
            pl.BlockSpec((d, n), const, pipeline_mode=pl.Buffered(1)),
        ],
        out_specs=[
            pl.BlockSpec((tm, n), lambda i: (i, 0)),
            pl.BlockSpec((tm, wk), lambda i: (i, 0)),
        ],
        out_shape=[
            jax.ShapeDtypeStruct((t, n), F32),
            jax.ShapeDtypeStruct((t, wk), F32),
        ],
        compiler_params=_params("parallel"),
        name="mix_proj",
    )(x, g, wg, wgk, bgk, wm)


def _attn_prompt_kernel(qt_ref, k_ref, vt_ref, lq1_ref, lk1_ref, lq2_ref, lk2_ref, gcol_ref,
                        o_ref, qs_ref, s_refs, c_refs, m_ref, l_ref, acc_ref, *, bq, bk, lam_init):
    n_buf = len(s_refs)
    ratio = bq // bk
    i = pl.program_id(1)
    q = qt_ref[...]
    row = lax.broadcasted_iota(jnp.int32, q.shape, 0)
    zero = jnp.zeros_like(q)
    qs_ref[0] = jnp.where(row < A_DQK, q, zero)
    qs_ref[1] = jnp.where(row >= A_DQK, q, zero)

    m_ref[...] = jnp.full(m_ref.shape, NEG_BIG, F32)
    l_ref[...] = jnp.zeros(l_ref.shape, F32)
    acc_ref[...] = jnp.zeros(acc_ref.shape, F32)

    def scores(kb, slot):
        start = pl.multiple_of(kb * bk, bk)
        k = k_ref[pl.ds(start, bk), :]
        for mi in range(2):
            s = jnp.dot(k, qs_ref[mi], preferred_element_type=F32)
            s_refs[slot][mi] = s
            c_refs[slot][mi] = jnp.max(s, axis=0, keepdims=True)

    def absorb(kb, slot, diag):
        start = pl.multiple_of(kb * bk, bk)
        vt = vt_ref[:, pl.ds(start, bk)]
        vt1 = jnp.concatenate([vt, jnp.ones((BF16_SUBLANES, bk), BF16)], axis=0)
        for mi in range(2):
            s = s_refs[slot][mi]
            if diag is not None:
                kpos = lax.broadcasted_iota(jnp.int32, s.shape, 0) + diag * bk
                qpos = lax.broadcasted_iota(jnp.int32, s.shape, 1)
                s = jnp.where(kpos <= qpos, s, NEG_BIG)
            m_old = m_ref[mi]
            cmax = c_refs[slot][mi] if diag is None else jnp.max(s, axis=0, keepdims=True)
            m_new = jnp.maximum(m_old, cmax)
            alpha = jnp.exp2(m_old - m_new)
            p = jnp.exp2(s - m_new).astype(BF16)
            upd = jnp.dot(vt1, p, preferred_element_type=F32)
            l_ref[mi] = alpha * l_ref[mi] + upd[A_DV:A_DV + 1]
            acc_ref[mi] = alpha * acc_ref[mi] + upd[:A_DV]
            m_ref[mi] = m_new

    ahead = ATTN_SCORE_LOOKAHEAD
    assert ahead <= ratio and ahead < n_buf
    for j in range(ahead):
        scores(j, j)

    n_plain = i * ratio
    n_full = n_plain // n_buf

    def full_round(r, carry):
        kb = r * n_buf
        for j in range(n_buf):
            scores(kb + j + ahead, (j + ahead) % n_buf)
            absorb(kb + j, j, None)
        return carry

    lax.fori_loop(0, n_full, full_round, 0)

    kb0 = n_full * n_buf
    rest = n_plain + ratio - kb0
    for count in range(ratio, ratio + n_buf):
        def tail(count=count):
            for j in range(count):
                if j + ahead < count:
                    scores(kb0 + j + ahead, (j + ahead) % n_buf)
                absorb(kb0 + j, j % n_buf, None if j < count - ratio else j - (count - ratio))
        pl.when(rest == count)(tail)

    lam = _lambda(lq1_ref[...], lk1_ref[...], lq2_ref[...], lk2_ref[...], lam_init)
    o = acc_ref[0] * (1.0 / l_ref[0]) - acc_ref[1] * (lam / l_ref[1])
    ms = jnp.mean(o * o, axis=0, keepdims=True)
    y = o * lax.rsqrt(ms + EPS) * gcol_ref[...] * (1.0 - lam_init)
    o_ref[...] = y.T


def _attn_prompt(qt, kb, vt, lq1, lk1, lq2, lk2, gcol, *, bq, bk, lam_init):
    d, t = qt.shape
    heads = d // A_DV
    vec = pl.BlockSpec((1, A_DQK), lambda h, i: (0, 0))
    return pl.pallas_call(
        functools.partial(_attn_prompt_kernel, bq=bq, bk=bk, lam_init=lam_init),
        grid=(heads, t // bq),
        in_specs=[
            pl.BlockSpec((A_DV, bq), lambda h, i: (h, i)),
            pl.BlockSpec((t, A_DV), lambda h, i: (0, h)),
            pl.BlockSpec((A_DV, t), lambda h, i: (h, 0)),
            vec, vec, vec, vec,
            pl.BlockSpec((A_DV, 1), lambda h, i: (0, 0)),
        ],
        out_specs=pl.BlockSpec((bq, A_DV), lambda h, i: (i, h)),
        out_shape=jax.ShapeDtypeStruct((t, d), F32),
        scratch_shapes=[
            pltpu.VMEM((2, A_DV, bq), BF16),
            tuple(pltpu.VMEM((2, bk, bq), F32) for _ in range(ATTN_SCORE_BUFFERS)),
            tuple(pltpu.VMEM((2, 1, bq), F32) for _ in range(ATTN_SCORE_BUFFERS)),
            pltpu.VMEM((2, 1, bq), F32),
            pltpu.VMEM((2, 1, bq), F32),
            pltpu.VMEM((2, A_DV, bq), F32),
        ],
        compiler_params=_params("parallel", "arbitrary"),
        name="attn_prompt",
    )(qt, kb, vt, lq1, lk1, lq2, lk2, gcol)


def _attn_decode_kernel(pt_ref, q_ref, kn_ref, vn_ref, lq1_ref, lk1_ref, lq2_ref, lk2_ref, g_ref,
                        *rest, n_group, n_tok, lam_init):
    k_refs = rest[:n_group]
    v_refs = rest[n_group:2 * n_group]
    o_ref, qbd_ref, m_ref, l_ref, acc_ref, kpage_ref, vpage_ref, kcat_ref, vcat_ref = rest[2 * n_group:]
    j = pl.program_id(1)
    rows, d = qbd_ref.shape
    page = kpage_ref.shape[0]
    heads = d // A_DV

    def update(k, v, mask):
        nt = (((1,), (1,)), ((), ()))
        s = lax.dot_general(qbd_ref[...], k, nt, preferred_element_type=F32)
        if mask is not None:
            s = jnp.where(mask, s, NEG_BIG)
        m_old = m_ref[...]
        m_new = jnp.maximum(m_old, jnp.max(s, axis=1, keepdims=True))
        alpha = jnp.exp2(m_old - m_new)
        p = jnp.exp2(s - m_new)
        l_ref[...] = alpha * l_ref[...] + jnp.sum(p, axis=1, keepdims=True)
        acc_ref[...] = alpha * acc_ref[...] + jnp.dot(p.astype(BF16), v, preferred_element_type=F32)
        m_ref[...] = m_new

    def gather_heads(ref, n_keys):
        cols = [ref[0, pl.ds(h, n_keys, stride=heads), :] for h in range(heads)]
        return jnp.concatenate(cols, axis=1).astype(BF16)

    @pl.when(j == 0)
    def _():
        q = q_ref[0]
        qrep = jnp.concatenate([q] * (rows // n_tok), axis=0)
        r = lax.broadcasted_iota(jnp.int32, (rows, d), 0)
        c = lax.broadcasted_iota(jnp.int32, (rows, d), 1)
        qbd_ref[...] = jnp.where(r // n_tok == c // A_DQK, qrep, jnp.zeros_like(qrep))
        m_ref[...] = jnp.full(m_ref.shape, NEG_BIG, F32)
        l_ref[...] = jnp.zeros(l_ref.shape, F32)
        acc_ref[...] = jnp.zeros(acc_ref.shape, F32)
        kpage_ref[...] = jnp.zeros(kpage_ref.shape, BF16)
        vpage_ref[...] = jnp.zeros(vpage_ref.shape, BF16)
        kpage_ref[0:n_tok, :] = gather_heads(kn_ref, n_tok)
        vpage_ref[0:n_tok, :] = gather_heads(vn_ref, n_tok)
        tq = lax.broadcasted_iota(jnp.int32, (rows, page), 0) % n_tok
        ts = lax.broadcasted_iota(jnp.int32, (rows, page), 1)
        update(kpage_ref[...], vpage_ref[...], ts <= tq)

    def load_pages(refs, dst_ref):
        for g_idx, ref in enumerate(refs):
            dst_ref[g_idx * page:(g_idx + 1) * page, :] = gather_heads(ref, page)
        return dst_ref[...]

    update(load_pages(k_refs, kcat_ref), load_pages(v_refs, vcat_ref), None)

    @pl.when(j == pl.num_programs(1) - 1)
    def _():
        lam = _lambda(lq1_ref[...], lk1_ref[...], lq2_ref[...], lk2_ref[...], lam_init)
        on = acc_ref[...] / l_ref[...]
        g = g_ref[...]
        for h in range(heads):
            r0 = h * 2 * n_tok
            c0 = h * A_DV
            o = on[r0:r0 + n_tok, c0:c0 + A_DV] - lam * on[r0 + n_tok:r0 + 2 * n_tok, c0:c0 + A_DV]
            o_ref[0, :, c0:c0 + A_DV] = _rms(o, g) * (1.0 - lam_init)


def _attn_decode(page_table, q, k_new, v_new, cache_k, cache_v, lq1, lk1, lq2, lk2, g, *, n_group, lam_init):
    bsz, n_tok, d = q.shape
    n_pages = page_table.shape[1]
    heads = d // A_DV
    page = cache_k.shape[1] // heads
    rows = (d // A_DQK) * n_tok
    vec = pl.BlockSpec((1, A_DQK), lambda b, j, pt: (0, 0))

    def page_spec(g_idx):
        return pl.BlockSpec((1, page * heads, A_DV), lambda b, j, pt: (pt[b, j * n_group + g_idx], 0, 0))

    grid_spec = pltpu.PrefetchScalarGridSpec(
        num_scalar_prefetch=1,
        grid=(bsz, n_pages // n_group),
        in_specs=[
            pl.BlockSpec((1, n_tok, d), lambda b, j, pt: (b, 0, 0)),
            pl.BlockSpec((1, n_tok * heads, A_DV), lambda b, j, pt: (b, 0, 0)),
            pl.BlockSpec((1, n_tok * heads, A_DV), lambda b, j, pt: (b, 0, 0)),
            vec, vec, vec, vec,
            pl.BlockSpec((1, A_DV), lambda b, j, pt: (0, 0)),
        ] + [page_spec(g_idx) for g_idx in range(n_group)] * 2,
        out_specs=pl.BlockSpec((1, n_tok, d), lambda b, j, pt: (b, 0, 0)),
        scratch_shapes=[
            pltpu.VMEM((rows, d), BF16),
            pltpu.VMEM((rows, 1), F32),
            pltpu.VMEM((rows, 1), F32),
            pltpu.VMEM((rows, d), F32),
            pltpu.VMEM((page, d), BF16),
            pltpu.VMEM((page, d), BF16),
            pltpu.VMEM((n_group * page, d), BF16),
            pltpu.VMEM((n_group * page, d), BF16),
        ],
    )
    return pl.pallas_call(
        functools.partial(_attn_decode_kernel, n_group=n_group, n_tok=n_tok, lam_init=lam_init),
        grid_spec=grid_spec,
        out_shape=jax.ShapeDtypeStruct((bsz, n_tok, d), F32),
        compiler_params=_params("parallel", "arbitrary"),
        name="attn_decode",
    )(page_table, q, k_new, v_new, lq1, lk1, lq2, lk2, g, *([cache_k] * n_group), *([cache_v] * n_group))


def _cumsum_rows(la):
    c = la.shape[0]
    t_i = lax.broadcasted_iota(jnp.int32, (c, c), 0)
    s_i = lax.broadcasted_iota(jnp.int32, (c, c), 1)
    tri = (s_i <= t_i).astype(F32)
    return jnp.dot(tri, la, precision=lax.Precision.HIGHEST, preferred_element_type=F32)


def _gla_chunk(q, k, v, b, st, *, sub, scale):
    c, dk = q.shape
    q = q * scale
    b = b * LOG2_E
    nt = (((1,), (1,)), ((), ()))

    o = lax.dot_general((q * jnp.exp2(b)).astype(BF16), st.astype(BF16), nt, preferred_element_type=F32)

    lane = lax.broadcasted_iota(jnp.int32, (sub, c), 1)
    trow = lax.broadcasted_iota(jnp.int32, (sub, c), 0)
    a_rows = []
    for blk in range(c // sub):
        r0 = blk * sub
        qi = q[r0:r0 + sub]
        bi = b[r0:r0 + sub]
        a_blk = jnp.zeros((sub, c), F32)
        if blk > 0:
            ref = b[r0:r0 + 1]
            qs = qi * jnp.exp2(bi - ref)
            ks = k * jnp.exp2(jnp.minimum(ref - b, 0.0))
            a_off = lax.dot_general(qs.astype(BF16), ks.astype(BF16), nt, preferred_element_type=F32)
            a_blk = jnp.where(lane < r0, a_off, a_blk)
        for s in range(sub):
            ks_row = k[r0 + s:r0 + s + 1]
            bs_row = b[r0 + s:r0 + s + 1]
            x = qi * ks_row * jnp.exp2(bi - bs_row)
            col = jnp.sum(x, axis=1, keepdims=True)
            a_blk = jnp.where((lane == r0 + s) & (trow >= s), col, a_blk)
        a_rows.append(a_blk)
    a = a_rows[0] if len(a_rows) == 1 else jnp.concatenate(a_rows, axis=0)
    o = o + jnp.dot(a.astype(BF16), v.astype(BF16), preferred_element_type=F32)

    bl = b[c - 1:c]
    kd = k * jnp.exp2(bl - b)
    tn = (((0,), (0,)), ((), ()))
    st_new = st * jnp.exp2(bl) + lax.dot_general(v.astype(BF16), kd.astype(BF16), tn, preferred_element_type=F32)
    return o, st_new


def _gla_prompt_kernel(qk_ref, v_ref, la_ref, x_ref, oa_ref, r_ref, ga_ref, gb_ref, gsub_ref, wo_ref, gpost_ref,
                       y_ref, s_ref, st_ref, o_ref, *, chunk, sub, scale):
    t = pl.program_id(0)
    heads, dv, dk = st_ref.shape

    @pl.when(t == 0)
    def _():
        st_ref[...] = jnp.zeros(st_ref.shape, F32)

    def body(ci, carry):
        r0 = pl.multiple_of(ci * chunk, chunk)
        rows = pl.ds(r0, chunk)
        b_all = _cumsum_rows(la_ref[rows, :])
        for h in range(heads):
            o, st_new = _gla_chunk(qk_ref[rows, h * dk:(h + 1) * dk],
                                   qk_ref[rows, (heads + h) * dk:(heads + h + 1) * dk],
                                   v_ref[rows, h * dv:(h + 1) * dv],
                                   b_all[:, h * dk:(h + 1) * dk], st_ref[h], sub=sub, scale=scale)
            o_ref[rows, h * dv:(h + 1) * dv] = o
            st_ref[h] = st_new
        return carry

    lax.fori_loop(0, qk_ref.shape[0] // chunk, body, 0)

    @pl.when(t == pl.num_programs(0) - 1)
    def _():
        for h in range(heads):
            s_ref[h] = st_ref[h].T

    y_ref[...] = _mix(x_ref[...], oa_ref[...], o_ref[...], r_ref[...], ga_ref[...], gb_ref[...],
                      gsub_ref[...], wo_ref[...], gpost_ref[...])


def _gla_prompt(x, oa, z, la, gsub, wo, gpost, *, tc, chunk, sub):
    t, d = x.shape
    dk = d // (2 * B_HEADS)
    dv = d // B_HEADS
    const = lambda i: (0, 0)
    tile = lambda j: pl.BlockSpec((tc, d), lambda i: (i, j))
    return pl.pallas_call(
        functools.partial(_gla_prompt_kernel, chunk=chunk, sub=sub, scale=dk ** -0.5),
        grid=(t // tc,),
        in_specs=[
            tile(0), tile(1),
            pl.BlockSpec((tc, B_HEADS * dk), lambda i: (i, 0)),
            tile(0), tile(0), tile(2), tile(3), tile(4),
            pl.BlockSpec((1, gsub.shape[1]), const),
            pl.BlockSpec((d, d), const),
            pl.BlockSpec((1, d), const),
        ],
        out_specs=[
            pl.BlockSpec((tc, d), lambda i: (i, 0)),
            pl.BlockSpec((B_HEADS, dk, dv), lambda i: (0, 0, 0)),
        ],
        out_shape=[
            jax.ShapeDtypeStruct((t, d), F32),
            jax.ShapeDtypeStruct((B_HEADS, dk, dv), F32),
        ],
        scratch_shapes=[pltpu.VMEM((B_HEADS, dv, dk), F32), pltpu.VMEM((tc, d), F32)],
        compiler_params=_params("arbitrary"),
        name="gla_prompt",
    )(z, z, la, x, oa, z, z, z, gsub, wo, gpost)


def _gla_sample_kernel(qk_ref, v_ref, la_ref, s0_ref, o_ref, s_ref, *, n_valid, scale):
    c = qk_ref.shape[1]
    heads, dk, dv = s0_ref.shape[1:]
    valid = lax.broadcasted_iota(jnp.int32, (c, 1), 0) < n_valid
    b_all = _cumsum_rows(jnp.where(valid, la_ref[0], 0.0))
    for h in range(heads):
        k = jnp.where(valid, qk_ref[0, :, (heads + h) * dk:(heads + h + 1) * dk], 0.0)
        o, st_new = _gla_chunk(qk_ref[0, :, h * dk:(h + 1) * dk], k, v_ref[0, :, h * dv:(h + 1) * dv],
                               b_all[:, h * dk:(h + 1) * dk], s0_ref[0, h].T, sub=c, scale=scale)
        o_ref[0, :, h * dv:(h + 1) * dv] = o
        s_ref[0, h] = st_new.T


def _gla_sample(z3, la3, s0, *, d, n_valid):
    bsz, c, _ = z3.shape
    state_spec = pl.BlockSpec((1,) + s0.shape[1:], lambda b: (b, 0, 0, 0))
    return pl.pallas_call(
        functools.partial(_gla_sample_kernel, n_valid=n_valid, scale=s0.shape[2] ** -0.5),
        grid=(bsz,),
        in_specs=[
            pl.BlockSpec((1, c, d), lambda b: (b, 0, 0)),
            pl.BlockSpec((1, c, d), lambda b: (b, 0, 1)),
            pl.BlockSpec((1, c, la3.shape[2]), lambda b: (b, 0, 0)),
            state_spec,
        ],
        out_specs=[pl.BlockSpec((1, c, d), lambda b: (b, 0, 0)), state_spec],
        out_shape=[
            jax.ShapeDtypeStruct((bsz, c, d), F32),
            jax.ShapeDtypeStruct(s0.shape, F32),
        ],
        compiler_params=_params("parallel"),
        name="gla_sample",
    )(z3, z3, la3, s0)


def _mix(x, oa, ob, r, ga, gb, gsub, wo, gpost):
    dv = gsub.shape[1]
    ob_n = jnp.concatenate(
        [_rms(ob[:, h * dv:(h + 1) * dv], gsub) for h in range(ob.shape[1] // dv)], axis=1)
    ob_n = ob_n * jax.nn.silu(r)
    merged = jax.nn.sigmoid(ga) * oa + jax.nn.sigmoid(gb) * ob_n
    y = jnp.dot(merged.astype(BF16), wo, preferred_element_type=F32)
    return x + _rms(y, gpost)


def _mix_out_kernel(x_ref, oa_ref, ob_ref, r_ref, ga_ref, gb_ref, gsub_ref, wo_ref, gpost_ref, y_ref):
    y_ref[...] = _mix(x_ref[...], oa_ref[...], ob_ref[...], r_ref[...], ga_ref[...], gb_ref[...],
                      gsub_ref[...], wo_ref[...], gpost_ref[...])


def _mix_out(x, oa, ob, z, gsub, wo, gpost, *, tm):
    t, d = x.shape
    row = lambda i: (i, 0)
    const = lambda i: (0, 0)
    return pl.pallas_call(
        _mix_out_kernel,
        grid=(t // tm,),
        in_specs=[
            pl.BlockSpec((tm, d), row),
            pl.BlockSpec((tm, d), row),
            pl.BlockSpec((tm, d), row),
            pl.BlockSpec((tm, d), lambda i: (i, 2)),
            pl.BlockSpec((tm, d), lambda i: (i, 3)),
            pl.BlockSpec((tm, d), lambda i: (i, 4)),
            pl.BlockSpec((1, gsub.shape[1]), const),
            pl.BlockSpec((d, d), const),
            pl.BlockSpec((1, d), const),
        ],
        out_specs=pl.BlockSpec((tm, d), row),
        out_shape=jax.ShapeDtypeStruct((t, d), F32),
        compiler_params=_params("parallel"),
        name="mix_out",
    )(x, oa, ob, z, z, z, gsub, wo, gpost)


def _ffn_kernel(x_ref, gpre_ref, wg_ref, wu_ref, wd_ref, gpost_ref, y_ref, h_ref, acc_ref):
    j = pl.program_id(1)

    @pl.when(j == 0)
    def _():
        h_ref[...] = _rms(x_ref[...], gpre_ref[...]).astype(BF16)
        acc_ref[...] = jnp.zeros(acc_ref.shape, F32)

    h = h_ref[...]
    gate = jnp.dot(h, wg_ref[...], preferred_element_type=F32)
    up = jnp.dot(h, wu_ref[...], preferred_element_type=F32)
    act = (jax.nn.silu(gate) * up).astype(BF16)
    acc_ref[...] += jnp.dot(act, wd_ref[...], preferred_element_type=F32)

    @pl.when(j == pl.num_programs(1) - 1)
    def _():
        y_ref[...] = x_ref[...] + _rms(acc_ref[...], gpost_ref[...])


def _ffn(x, gpre, wg, wu, wd, gpost, *, tm, tf):
    t, d = x.shape
    dff = wg.shape[1]
    row = lambda i, j: (i, 0)
    const = lambda i, j: (0, 0)
    return pl.pallas_call(
        _ffn_kernel,
        grid=(t // tm, dff // tf),
        in_specs=[
            pl.BlockSpec((tm, d), row),
            pl.BlockSpec((1, d), const),
            pl.BlockSpec((d, tf), lambda i, j: (0, j)),
            pl.BlockSpec((d, tf), lambda i, j: (0, j)),
            pl.BlockSpec((tf, d), lambda i, j: (j, 0)),
            pl.BlockSpec((1, d), const),
        ],
        out_specs=pl.BlockSpec((tm, d), row),
        out_shape=jax.ShapeDtypeStruct((t, d), F32),
        scratch_shapes=[pltpu.VMEM((tm, d), BF16), pltpu.VMEM((tm, d), F32)],
        compiler_params=_params("parallel", "arbitrary"),
        name="ffn",
    )(x, gpre, wg, wu, wd, gpost)


def _largest_divisor(n, candidates):
    for c in candidates:
        if n % c == 0:
            return c
    raise ValueError(f"no supported tile for extent {n}")


def _prep_weights(g_mix_pre, w_in, w_gk, b_gk, g_attn_sub, g_gla_sub, w_out, g_mix_post,
                  g_ffn_pre, w_gate, w_up, w_down, g_ffn_post):
    d = w_in.shape[0]
    rank = w_gk.shape[0]
    a_w, wk, wv = d, d // 2, d
    off = [0]
    for s in (a_w, a_w, a_w, wk, wk, wv, rank, wv, d, d):
        off.append(off[-1] + s)
    col = lambda a, b: w_in[:, off[a]:off[b]]
    wq_t = col(0, 1).T.astype(BF16)
    wv_t = col(2, 3).T.astype(BF16)
    wm = jnp.concatenate([col(3, 5), col(5, 6), col(7, 8), col(8, 9), col(9, 10)], axis=1).astype(BF16)
    wg = jnp.pad(col(6, 7), ((0, 0), (0, LANES - rank))).astype(BF16)
    wgk = jnp.pad(w_gk, ((0, LANES - rank), (0, 0))).astype(BF16)
    row = lambda v: v.reshape(1, -1).astype(F32)
    return dict(
        g_mix_pre=row(g_mix_pre), wq_t=wq_t, wv_t=wv_t, wk=col(1, 2).astype(BF16), wv=col(2, 3).astype(BF16),
        wm=wm, wg=wg, wgk=wgk, bgk=row(b_gk),
        g_attn_row=row(g_attn_sub), g_attn_col=g_attn_sub.reshape(-1, 1).astype(F32),
        g_gla_sub=row(g_gla_sub), w_out=w_out.astype(BF16), g_mix_post=row(g_mix_post),
        g_ffn_pre=row(g_ffn_pre), w_gate=w_gate.astype(BF16), w_up=w_up.astype(BF16),
        w_down=w_down.astype(BF16), g_ffn_post=row(g_ffn_post))


def _ffn_block(x1, w):
    t = x1.shape[0]
    dff = w["w_gate"].shape[1]
    tf = dff // 2 if (dff // 2) % LANES == 0 else dff
    tmf = _largest_divisor(t, (1024, 512, 256, 128, 64, 32, 16, 8))
    return _ffn(x1, w["g_ffn_pre"], w["w_gate"], w["w_up"], w["w_down"], w["g_ffn_post"], tm=tmf, tf=tf)


def kernel(x_prompt, x_sample, cache_k, cache_v, state_gla, page_table, g_mix_pre, w_in, w_gk, b_gk,
           lambda_q1, lambda_k1, lambda_q2, lambda_k2, g_attn_sub, g_gla_sub, w_out, g_mix_post,
           g_ffn_pre, w_gate, w_up, w_down, g_ffn_post):
    depth = w_in.shape[0]
    bsz, seq, d = x_prompt.shape
    dbsz, dseq, _ = x_sample.shape
    heads = d // A_DV
    page = cache_k.shape[2]
    n_pool = cache_k.shape[1]
    n_pages = page_table.shape[1]
    dk = d // (2 * B_HEADS)
    dv = d // B_HEADS
    assert bsz == 1, "prompt kernels take one sequence"
    pad_seq = -(-dseq // SUBLANES) * SUBLANES

    x_p = x_prompt.reshape(seq, d)
    x_s = jnp.pad(x_sample, ((0, 0), (0, pad_seq - dseq), (0, 0))).reshape(dbsz * pad_seq, d)

    outs = ([], [], [], [], [], [])
    for l in range(depth):
        lam_init = 0.8 - 0.6 * math.exp(-0.3 * l)
        w = _prep_weights(g_mix_pre[l], w_in[l], w_gk[l], b_gk[l], g_attn_sub[l], g_gla_sub[l], w_out[l],
                          g_mix_post[l], g_ffn_pre[l], w_gate[l], w_up[l], w_down[l], g_ffn_post[l])
        lams = [v[l].reshape(1, A_DQK).astype(F32) for v in (lambda_q1, lambda_k1, lambda_q2, lambda_k2)]
        attn_proj = functools.partial(_attn_proj, g=w["g_mix_pre"], wq_t=w["wq_t"], wv_t=w["wv_t"],
                                      wk=w["wk"], wv=w["wv"])
        mix_proj = functools.partial(_mix_proj, g=w["g_mix_pre"], wg=w["wg"], wgk=w["wgk"], bgk=w["bgk"],
                                     wm=w["wm"])

        qt, vt, kb, k_p, v_p = attn_proj(x_p, tm=_largest_divisor(seq, (512, 256, 128)))
        z, la = mix_proj(x_p, tm=_largest_divisor(seq, (512, 256, 128)))
        blk = _largest_divisor(seq, (512, 256, 128))
        oa = _attn_prompt(qt, kb, vt, *lams, w["g_attn_col"], bq=blk, bk=blk, lam_init=lam_init)
        chunk = _largest_divisor(seq, (128, 64, 32, 16, 8))
        tc = _largest_divisor(seq, (512, 256, 128, 64, 32, 16, 8))
        x_p, s_p = _gla_prompt(x_p, oa, z, la, w["g_gla_sub"], w["w_out"], w["g_mix_post"],
                               tc=tc, chunk=chunk, sub=min(16, chunk))
        x_p = _ffn_block(x_p, w)
        outs[0].append(k_p.reshape(bsz, seq, heads, A_DV))
        outs[1].append(v_p.reshape(bsz, seq, heads, A_DV))
        outs[2].append(s_p.reshape(bsz, B_HEADS, dk, dv))

        ts = dbsz * pad_seq
        tms = _largest_divisor(ts, (256, 128))
        qts, _, _, k_s, v_s = attn_proj(x_s, tm=tms)
        zs, las = mix_proj(x_s, tm=tms)
        zs3 = zs.reshape(dbsz, pad_seq, zs.shape[1])
        q_s = qts.T.reshape(dbsz, pad_seq, d)
        k_s = k_s.reshape(dbsz, pad_seq * heads, A_DV)
        v_s = v_s.reshape(dbsz, pad_seq * heads, A_DV)
        n_group = _largest_divisor(n_pages, (16, 8, 4, 2, 1))
        oas = _attn_decode(page_table, q_s, k_s, v_s, cache_k[l].reshape(n_pool, page * heads, A_DV),
                           cache_v[l].reshape(n_pool, page * heads, A_DV), *lams, w["g_attn_row"],
                           n_group=n_group, lam_init=lam_init)
        obs, s_s = _gla_sample(zs3, las.reshape(dbsz, pad_seq, las.shape[1]), state_gla[l], d=d, n_valid=dseq)
        x_s = _mix_out(x_s, oas.reshape(ts, d), obs.reshape(ts, d), zs, w["g_gla_sub"], w["w_out"],
                       w["g_mix_post"], tm=_largest_divisor(ts, (512, 256, 128, 64, 32, 16, 8)))
        x_s = _ffn_block(x_s, w)
        outs[3].append(k_s.reshape(dbsz, pad_seq, heads, A_DV)[:, :dseq])
        outs[4].append(v_s.reshape(dbsz, pad_seq, heads, A_DV)[:, :dseq])
        outs[5].append(s_s)

    y_p = x_p.reshape(bsz, seq, d)
    y_s = x_s.reshape(dbsz, pad_seq, d)[:, :dseq]
    k_p, v_p, g_p, k_s, v_s, g_s = (jnp.stack(o) for o in outs)
    return (y_p, y_s, k_p, v_p, g_p, k_s, v_s, g_s)
```

```python
import functools
import math

import jax
import jax.numpy as jnp
from jax import lax
from jax.experimental import pallas as pl
from jax.experimental.pallas import tpu as pltpu

A_DQK = 64
A_DV = 2 * A_DQK
B_HEADS = 4
GATE_TAU = 16.0
EPS = 1e-6

LANES = 128
SUBLANES = 8
BF16_SUBLANES = 16
VMEM_LIMIT_BYTES = 56 * 1024 * 1024

NEG_BIG = -1e30
LOG2_E = math.log2(math.e)
ATTN_SCORE_BUFFERS = 8
ATTN_SCORE_LOOKAHEAD = 1
F32 = jnp.float32
BF16 = jnp.bfloat16


def _params(*semantics):
    return pltpu.CompilerParams(dimension_semantics=semantics, vmem_limit_bytes=VMEM_LIMIT_BYTES)


def _rms(x, g):
    return x * lax.rsqrt(jnp.mean(x * x, axis=-1, keepdims=True) + EPS) * g


def _lambda(lq1, lk1, lq2, lk2, lam_init):
    s1 = jnp.sum(lq1 * lk1, axis=-1, keepdims=True)
    s2 = jnp.sum(lq2 * lk2, axis=-1, keepdims=True)
    return jnp.exp(s1) - jnp.exp(s2) + lam_init


def _attn_proj_kernel(x_ref, g_ref, wq_t_ref, wv_t_ref, wk_ref, wv_ref,
                      qt_ref, vt_ref, kb_ref, ko_ref, vo_ref, *, q_scale, heads):
    tm = x_ref.shape[0]
    h = _rms(x_ref[...], g_ref[...]).astype(BF16)
    nt = (((1,), (1,)), ((), ()))
    qt = lax.dot_general(wq_t_ref[...], h, nt, preferred_element_type=F32)
    qt_ref[...] = (qt * q_scale).astype(BF16)
    vt = lax.dot_general(wv_t_ref[...], h, nt, preferred_element_type=F32)
    vt_ref[...] = vt.astype(BF16)
    k = jnp.dot(h, wk_ref[...], preferred_element_type=F32)
    kb_ref[...] = k.astype(BF16)
    v = jnp.dot(h, wv_ref[...], preferred_element_type=F32)
    for hd in range(heads):
        ko_ref[pl.ds(hd, tm, stride=heads), :] = k[:, hd * A_DV:(hd + 1) * A_DV]
        vo_ref[pl.ds(hd, tm, stride=heads), :] = v[:, hd * A_DV:(hd + 1) * A_DV]


def _attn_proj(x, g, wq_t, wv_t, wk, wv, *, tm):
    t, d = x.shape
    heads = d // A_DV
    const = lambda i: (0, 0)
    return pl.pallas_call(
        functools.partial(_attn_proj_kernel, q_scale=A_DQK ** -0.5 * LOG2_E, heads=heads),
        grid=(t // tm,),
        in_specs=[
            pl.BlockSpec((tm, d), lambda i: (i, 0)),
            pl.BlockSpec((1, d), const),
            pl.BlockSpec((d, d), const),
            pl.BlockSpec((d, d), const),
            pl.BlockSpec((d, d), const),
            pl.BlockSpec((d, d), const),
        ],
        out_specs=[
            pl.BlockSpec((d, tm), lambda i: (0, i)),
            pl.BlockSpec((d, tm), lambda i: (0, i)),
            pl.BlockSpec((tm, d), lambda i: (i, 0)),
            pl.BlockSpec((tm * heads, A_DV), lambda i: (i, 0)),
            pl.BlockSpec((tm * heads, A_DV), lambda i: (i, 0)),
        ],
        out_shape=[
            jax.ShapeDtypeStruct((d, t), BF16),
            jax.ShapeDtypeStruct((d, t), BF16),
            jax.ShapeDtypeStruct((t, d), BF16),
            jax.ShapeDtypeStruct((t * heads, A_DV), F32),
            jax.ShapeDtypeStruct((t * heads, A_DV), F32),
        ],
        compiler_params=_params("parallel"),
        name="attn_proj",
    )(x, g, wq_t, wv_t, wk, wv)


def _mix_proj_kernel(x_ref, g_ref, wg_ref, wgk_ref, bgk_ref, wm_ref, z_ref, la_ref):
    h = _rms(x_ref[...], g_ref[...]).astype(BF16)
    g_lr = jnp.dot(h, wg_ref[...], preferred_element_type=F32)
    gk = jnp.dot(g_lr.astype(BF16), wgk_ref[...], preferred_element_type=F32) + bgk_ref[...]
    la_ref[...] = jax.nn.log_sigmoid(gk) / GATE_TAU
    z_ref[...] = jnp.dot(h, wm_ref[...], preferred_element_type=F32)


def _mix_proj(x, g, wg, wgk, bgk, wm, *, tm):
    t, d = x.shape
    n = wm.shape[1]
    wk = wgk.shape[1]
    const = lambda i: (0, 0)
    return pl.pallas_call(
        _mix_proj_kernel,
        grid=(t // tm,),
        in_specs=[
            pl.BlockSpec((tm, d), lambda i: (i, 0)),
            pl.BlockSpec((1, d), const),
            pl.BlockSpec((d, LANES), const),
            pl.BlockSpec((LANES, wk), const),
            pl.BlockSpec((1, wk), const),
            pl.BlockSpec((d, n), const, pipeline_mode=pl.Buffered(1)),
        ],
        out_specs=[
            pl.BlockSpec((tm, n), lambda i: (i, 0)),
            pl.BlockSpec((tm, wk), lambda i: (i, 0)),
        ],
        out_shape=[
            jax.ShapeDtypeStruct((t, n), F32),
            jax.ShapeDtypeStruct((t, wk), F32),
        ],
        compiler_params=_params("parallel"),
        name="mix_proj",
    )(x, g, wg, wgk, bgk, wm)


def _attn_prompt_kernel(qt_ref, k_ref, vt_ref, lq1_ref, lk1_ref, lq2_ref, lk2_ref, gcol_ref,
                        o_ref, qs_ref, s_refs, c_refs, m_ref, l_ref, acc_ref, *, bq, bk, lam_init):
    n_buf = len(s_refs)
    ratio = bq // bk
    i = pl.program_id(1)
    q = qt_ref[...]
    row = lax.broadcasted_iota(jnp.int32, q.shape, 0)
    zero = jnp.zeros_like(q)
    qs_ref[0] = jnp.where(row < A_DQK, q, zero)
    qs_ref[1] = jnp.where(row >= A_DQK, q, zero)

    m_ref[...] = jnp.full(m_ref.shape, NEG_BIG, F32)
    l_ref[...] = jnp.zeros(l_ref.shape, F32)
    acc_ref[...] = jnp.zeros(acc_ref.shape, F32)

    def scores(kb, slot):
        start = pl.multiple_of(kb * bk, bk)
        k = k_ref[pl.ds(start, bk), :]
        for mi in range(2):
            s = jnp.dot(k, qs_ref[mi], preferred_element_type=F32)
            s_refs[slot][mi] = s
            c_refs[slot][mi] = jnp.max(s, axis=0, keepdims=True)

    def absorb(kb, slot, diag):
        start = pl.multiple_of(kb * bk, bk)
        vt = vt_ref[:, pl.ds(start, bk)]
        vt1 = jnp.concatenate([vt, jnp.ones((BF16_SUBLANES, bk), BF16)], axis=0)
        for mi in range(2):
            s = s_refs[slot][mi]
            if diag is not None:
                kpos = lax.broadcasted_iota(jnp.int32, s.shape, 0) + diag * bk
                qpos = lax.broadcasted_iota(jnp.int32, s.shape, 1)
                s = jnp.where(kpos <= qpos, s, NEG_BIG)
            m_old = m_ref[mi]
            cmax = c_refs[slot][mi] if diag is None else jnp.max(s, axis=0, keepdims=True)
            m_new = jnp.maximum(m_old, cmax)
            alpha = jnp.exp2(m_old - m_new)
            p = jnp.exp2(s - m_new).astype(BF16)
            upd = jnp.dot(vt1, p, preferred_element_type=F32)
            l_ref[mi] = alpha * l_ref[mi] + upd[A_DV:A_DV + 1]
            acc_ref[mi] = alpha * acc_ref[mi] + upd[:A_DV]
            m_ref[mi] = m_new

    ahead = ATTN_SCORE_LOOKAHEAD
    assert ahead <= ratio and ahead < n_buf
    for j in range(ahead):
        scores(j, j)

    n_plain = i * ratio
    n_full = n_plain // n_buf

    def full_round(r, carry):
        kb = r * n_buf
        for j in range(n_buf):
            scores(kb + j + ahead, (j + ahead) % n_buf)
            absorb(kb + j, j, None)
        return carry

    lax.fori_loop(0, n_full, full_round, 0)

    kb0 = n_full * n_buf
    rest = n_plain + ratio - kb0
    for count in range(ratio, ratio + n_buf):
        def tail(count=count):
            for j in range(count):
                if j + ahead < count:
                    scores(kb0 + j + ahead, (j + ahead) % n_buf)
                absorb(kb0 + j, j % n_buf, None if j < count - ratio else j - (count - ratio))
        pl.when(rest == count)(tail)

    lam = _lambda(lq1_ref[...], lk1_ref[...], lq2_ref[...], lk2_ref[...], lam_init)
    o = acc_ref[0] * (1.0 / l_ref[0]) - acc_ref[1] * (lam / l_ref[1])
    ms = jnp.mean(o * o, axis=0, keepdims=True)
    y = o * lax.rsqrt(ms + EPS) * gcol_ref[...] * (1.0 - lam_init)
    o_ref[...] = y.T


def _attn_prompt(qt, kb, vt, lq1, lk1, lq2, lk2, gcol, *, bq, bk, lam_init):
    d, t = qt.shape
    heads = d // A_DV
    vec = pl.BlockSpec((1, A_DQK), lambda h, i: (0, 0))
    return pl.pallas_call(
        functools.partial(_attn_prompt_kernel, bq=bq, bk=bk, lam_init=lam_init),
        grid=(heads, t // bq),
        in_specs=[
            pl.BlockSpec((A_DV, bq), lambda h, i: (h, i)),
            pl.BlockSpec((t, A_DV), lambda h, i: (0, h)),
            pl.BlockSpec((A_DV, t), lambda h, i: (h, 0)),
            vec, vec, vec, vec,
            pl.BlockSpec((A_DV, 1), lambda h, i: (0, 0)),
        ],
        out_specs=pl.BlockSpec((bq, A_DV), lambda h, i: (i, h)),
        out_shape=jax.ShapeDtypeStruct((t, d), F32),
        scratch_shapes=[
            pltpu.VMEM((2, A_DV, bq), BF16),
            tuple(pltpu.VMEM((2, bk, bq), F32) for _ in range(ATTN_SCORE_BUFFERS)),
            tuple(pltpu.VMEM((2, 1, bq), F32) for _ in range(ATTN_SCORE_BUFFERS)),
            pltpu.VMEM((2, 1, bq), F32),
            pltpu.VMEM((2, 1, bq), F32),
            pltpu.VMEM((2, A_DV, bq), F32),
        ],
        compiler_params=_params("parallel", "arbitrary"),
        name="attn_prompt",
    )(qt, kb, vt, lq1, lk1, lq2, lk2, gcol)


def _attn_decode_kernel(pt_ref, q_ref, kn_ref, vn_ref, lq1_ref, lk1_ref, lq2_ref, lk2_ref, g_ref,
                        *rest, n_group, n_tok, lam_init):
    k_refs = rest[:n_group]
    v_refs = rest[n_group:2 * n_group]
    o_ref, qbd_ref, m_ref, l_ref, acc_ref, kpage_ref, vpage_ref, kcat_ref, vcat_ref = rest[2 * n_group:]
    j = pl.program_id(1)
    rows, d = qbd_ref.shape
    page = kpage_ref.shape[0]
    heads = d // A_DV

    def update(k, v, mask):
        nt = (((1,), (1,)), ((), ()))
        s = lax.dot_general(qbd_ref[...], k, nt, preferred_element_type=F32)
        if mask is not None:
            s = jnp.where(mask, s, NEG_BIG)
        m_old = m_ref[...]
        m_new = jnp.maximum(m_old, jnp.max(s, axis=1, keepdims=True))
        alpha = jnp.exp2(m_old - m_new)
        p = jnp.exp2(s - m_new)
        l_ref[...] = alpha * l_ref[...] + jnp.sum(p, axis=1, keepdims=True)
        acc_ref[...] = alpha * acc_ref[...] + jnp.dot(p.astype(BF16), v, preferred_element_type=F32)
        m_ref[...] = m_new

    def gather_heads(ref, n_keys):
        cols = [ref[0, pl.ds(h, n_keys, stride=heads), :] for h in range(heads)]
        return jnp.concatenate(cols, axis=1).astype(BF16)

    @pl.when(j == 0)
    def _():
        q = q_ref[0]
        qrep = jnp.concatenate([q] * (rows // n_tok), axis=0)
        r = lax.broadcasted_iota(jnp.int32, (rows, d), 0)
        c = lax.broadcasted_iota(jnp.int32, (rows, d), 1)
        qbd_ref[...] = jnp.where(r // n_tok == c // A_DQK, qrep, jnp.zeros_like(qrep))
        m_ref[...] = jnp.full(m_ref.shape, NEG_BIG, F32)
        l_ref[...] = jnp.zeros(l_ref.shape, F32)
        acc_ref[...] = jnp.zeros(acc_ref.shape, F32)
        kpage_ref[...] = jnp.zeros(kpage_ref.shape, BF16)
        vpage_ref[...] = jnp.zeros(vpage_ref.shape, BF16)
        kpage_ref[0:n_tok, :] = gather_heads(kn_ref, n_tok)
        vpage_ref[0:n_tok, :] = gather_heads(vn_ref, n_tok)
        tq = lax.broadcasted_iota(jnp.int32, (rows, page), 0) % n_tok
        ts = lax.broadcasted_iota(jnp.int32, (rows, page), 1)
        update(kpage_ref[...], vpage_ref[...], ts <= tq)

    def load_pages(refs, dst_ref):
        for g_idx, ref in enumerate(refs):
            dst_ref[g_idx * page:(g_idx + 1) * page, :] = gather_heads(ref, page)
        return dst_ref[...]

    update(load_pages(k_refs, kcat_ref), load_pages(v_refs, vcat_ref), None)

    @pl.when(j == pl.num_programs(1) - 1)
    def _():
        lam = _lambda(lq1_ref[...], lk1_ref[...], lq2_ref[...], lk2_ref[...], lam_init)
        on = acc_ref[...] / l_ref[...]
        g = g_ref[...]
        for h in range(heads):
            r0 = h * 2 * n_tok
            c0 = h * A_DV
            o = on[r0:r0 + n_tok, c0:c0 + A_DV] - lam * on[r0 + n_tok:r0 + 2 * n_tok, c0:c0 + A_DV]
            o_ref[0, :, c0:c0 + A_DV] = _rms(o, g) * (1.0 - lam_init)


def _attn_decode(page_table, q, k_new, v_new, cache_k, cache_v, lq1, lk1, lq2, lk2, g, *, n_group, lam_init):
    bsz, n_tok, d = q.shape
    n_pages = page_table.shape[1]
    heads = d // A_DV
    page = cache_k.shape[1] // heads
    rows = (d // A_DQK) * n_tok
    vec = pl.BlockSpec((1, A_DQK), lambda b, j, pt: (0, 0))

    def page_spec(g_idx):
        return pl.BlockSpec((1, page * heads, A_DV), lambda b, j, pt: (pt[b, j * n_group + g_idx], 0, 0))

    grid_spec = pltpu.PrefetchScalarGridSpec(
        num_scalar_prefetch=1,
        grid=(bsz, n_pages // n_group),
        in_specs=[
            pl.BlockSpec((1, n_tok, d), lambda b, j, pt: (b, 0, 0)),
            pl.BlockSpec((1, n_tok * heads, A_DV), lambda b, j, pt: (b, 0, 0)),
            pl.BlockSpec((1, n_tok * heads, A_DV), lambda b, j, pt: (b, 0, 0)),
            vec, vec, vec, vec,
            pl.BlockSpec((1, A_DV), lambda b, j, pt: (0, 0)),
        ] + [page_spec(g_idx) for g_idx in range(n_group)] * 2,
        out_specs=pl.BlockSpec((1, n_tok, d), lambda b, j, pt: (b, 0, 0)),
        scratch_shapes=[
            pltpu.VMEM((rows, d), BF16),
            pltpu.VMEM((rows, 1), F32),
            pltpu.VMEM((rows, 1), F32),
            pltpu.VMEM((rows, d), F32),
            pltpu.VMEM((page, d), BF16),
            pltpu.VMEM((page, d), BF16),
            pltpu.VMEM((n_group * page, d), BF16),
            pltpu.VMEM((n_group * page, d), BF16),
        ],
    )
    return pl.pallas_call(
        functools.partial(_attn_decode_kernel, n_group=n_group, n_tok=n_tok, lam_init=lam_init),
        grid_spec=grid_spec,
        out_shape=jax.ShapeDtypeStruct((bsz, n_tok, d), F32),
        compiler_params=_params("parallel", "arbitrary"),
        name="attn_decode",
    )(page_table, q, k_new, v_new, lq1, lk1, lq2, lk2, g, *([cache_k] * n_group), *([cache_v] * n_group))


def _cumsum_rows(la):
    c = la.shape[0]
    t_i = lax.broadcasted_iota(jnp.int32, (c, c), 0)
    s_i = lax.broadcasted_iota(jnp.int32, (c, c), 1)
    tri = (s_i <= t_i).astype(F32)
    return jnp.dot(tri, la, precision=lax.Precision.HIGHEST, preferred_element_type=F32)


def _gla_chunk(q, k, v, b, st, *, sub, scale):
    c, dk = q.shape
    q = q * scale
    b = b * LOG2_E
    nt = (((1,), (1,)), ((), ()))

    o = lax.dot_general((q * jnp.exp2(b)).astype(BF16), st.astype(BF16), nt, preferred_element_type=F32)

    lane = lax.broadcasted_iota(jnp.int32, (sub, c), 1)
    trow = lax.broadcasted_iota(jnp.int32, (sub, c), 0)
    a_rows = []
    for blk in range(c // sub):
        r0 = blk * sub
        qi = q[r0:r0 + sub]
        bi = b[r0:r0 + sub]
        a_blk = jnp.zeros((sub, c), F32)
        if blk > 0:
            ref = b[r0:r0 + 1]
            qs = qi * jnp.exp2(bi - ref)
            ks = k * jnp.exp2(jnp.minimum(ref - b, 0.0))
            a_off = lax.dot_general(qs.astype(BF16), ks.astype(BF16), nt, preferred_element_type=F32)
            a_blk = jnp.where(lane < r0, a_off, a_blk)
        for s in range(sub):
            ks_row = k[r0 + s:r0 + s + 1]
            bs_row = b[r0 + s:r0 + s + 1]
            x = qi * ks_row * jnp.exp2(bi - bs_row)
            col = jnp.sum(x, axis=1, keepdims=True)
            a_blk = jnp.where((lane == r0 + s) & (trow >= s), col, a_blk)
        a_rows.append(a_blk)
    a = a_rows[0] if len(a_rows) == 1 else jnp.concatenate(a_rows, axis=0)
    o = o + jnp.dot(a.astype(BF16), v.astype(BF16), preferred_element_type=F32)

    bl = b[c - 1:c]
    kd = k * jnp.exp2(bl - b)
    tn = (((0,), (0,)), ((), ()))
    st_new = st * jnp.exp2(bl) + lax.dot_general(v.astype(BF16), kd.astype(BF16), tn, preferred_element_type=F32)
    return o, st_new


def _gla_prompt_kernel(qk_ref, v_ref, la_ref, x_ref, oa_ref, r_ref, ga_ref, gb_ref, gsub_ref, wo_ref, gpost_ref,
                       y_ref, s_ref, st_ref, o_ref, *, chunk, sub, scale):
    t = pl.program_id(0)
    heads, dv, dk = st_ref.shape

    @pl.when(t == 0)
    def _():
        st_ref[...] = jnp.zeros(st_ref.shape, F32)

    def body(ci, carry):
        r0 = pl.multiple_of(ci * chunk, chunk)
        rows = pl.ds(r0, chunk)
        b_all = _cumsum_rows(la_ref[rows, :])
        for h in range(heads):
            o, st_new = _gla_chunk(qk_ref[rows, h * dk:(h + 1) * dk],
                                   qk_ref[rows, (heads + h) * dk:(heads + h + 1) * dk],
                                   v_ref[rows, h * dv:(h + 1) * dv],
                                   b_all[:, h * dk:(h + 1) * dk], st_ref[h], sub=sub, scale=scale)
            o_ref[rows, h * dv:(h + 1) * dv] = o
            st_ref[h] = st_new
        return carry

    lax.fori_loop(0, qk_ref.shape[0] // chunk, body, 0)

    @pl.when(t == pl.num_programs(0) - 1)
    def _():
        for h in range(heads):
            s_ref[h] = st_ref[h].T

    y_ref[...] = _mix(x_ref[...], oa_ref[...], o_ref[...], r_ref[...], ga_ref[...], gb_ref[...],
                      gsub_ref[...], wo_ref[...], gpost_ref[...])


def _gla_prompt(x, oa, z, la, gsub, wo, gpost, *, tc, chunk, sub):
    t, d = x.shape
    dk = d // (2 * B_HEADS)
    dv = d // B_HEADS
    const = lambda i: (0, 0)
    tile = lambda j: pl.BlockSpec((tc, d), lambda i: (i, j))
    return pl.pallas_call(
        functools.partial(_gla_prompt_kernel, chunk=chunk, sub=sub, scale=dk ** -0.5),
        grid=(t // tc,),
        in_specs=[
            tile(0), tile(1),
            pl.BlockSpec((tc, B_HEADS * dk), lambda i: (i, 0)),
            tile(0), tile(0), tile(2), tile(3), tile(4),
            pl.BlockSpec((1, gsub.shape[1]), const),
            pl.BlockSpec((d, d), const),
            pl.BlockSpec((1, d), const),
        ],
        out_specs=[
            pl.BlockSpec((tc, d), lambda i: (i, 0)),
            pl.BlockSpec((B_HEADS, dk, dv), lambda i: (0, 0, 0)),
        ],
        out_shape=[
            jax.ShapeDtypeStruct((t, d), F32),
            jax.ShapeDtypeStruct((B_HEADS, dk, dv), F32),
        ],
        scratch_shapes=[pltpu.VMEM((B_HEADS, dv, dk), F32), pltpu.VMEM((tc, d), F32)],
        compiler_params=_params("arbitrary"),
        name="gla_prompt",
    )(z, z, la, x, oa, z, z, z, gsub, wo, gpost)


def _gla_sample_kernel(qk_ref, v_ref, la_ref, s0_ref, o_ref, s_ref, *, n_valid, scale):
    c = qk_ref.shape[1]
    heads, dk, dv = s0_ref.shape[1:]
    valid = lax.broadcasted_iota(jnp.int32, (c, 1), 0) < n_valid
    b_all = _cumsum_rows(jnp.where(valid, la_ref[0], 0.0))
    for h in range(heads):
        k = jnp.where(valid, qk_ref[0, :, (heads + h) * dk:(heads + h + 1) * dk], 0.0)
        o, st_new = _gla_chunk(qk_ref[0, :, h * dk:(h + 1) * dk], k, v_ref[0, :, h * dv:(h + 1) * dv],
                               b_all[:, h * dk:(h + 1) * dk], s0_ref[0, h].T, sub=c, scale=scale)
        o_ref[0, :, h * dv:(h + 1) * dv] = o
        s_ref[0, h] = st_new.T


def _gla_sample(z3, la3, s0, *, d, n_valid):
    bsz, c, _ = z3.shape
    state_spec = pl.BlockSpec((1,) + s0.shape[1:], lambda b: (b, 0, 0, 0))
    return pl.pallas_call(
        functools.partial(_gla_sample_kernel, n_valid=n_valid, scale=s0.shape[2] ** -0.5),
        grid=(bsz,),
        in_specs=[
            pl.BlockSpec((1, c, d), lambda b: (b, 0, 0)),
            pl.BlockSpec((1, c, d), lambda b: (b, 0, 1)),
            pl.BlockSpec((1, c, la3.shape[2]), lambda b: (b, 0, 0)),
            state_spec,
        ],
        out_specs=[pl.BlockSpec((1, c, d), lambda b: (b, 0, 0)), state_spec],
        out_shape=[
            jax.ShapeDtypeStruct((bsz, c, d), F32),
            jax.ShapeDtypeStruct(s0.shape, F32),
        ],
        compiler_params=_params("parallel"),
        name="gla_sample",
    )(z3, z3, la3, s0)


def _mix(x, oa, ob, r, ga, gb, gsub, wo, gpost):
    dv = gsub.shape[1]
    ob_n = jnp.concatenate(
        [_rms(ob[:, h * dv:(h + 1) * dv], gsub) for h in range(ob.shape[1] // dv)], axis=1)
    ob_n = ob_n * jax.nn.silu(r)
    merged = jax.nn.sigmoid(ga) * oa + jax.nn.sigmoid(gb) * ob_n
    y = jnp.dot(merged.astype(BF16), wo, preferred_element_type=F32)
    return x + _rms(y, gpost)


def _mix_out_kernel(x_ref, oa_ref, ob_ref, r_ref, ga_ref, gb_ref, gsub_ref, wo_ref, gpost_ref, y_ref):
    y_ref[...] = _mix(x_ref[...], oa_ref[...], ob_ref[...], r_ref[...], ga_ref[...], gb_ref[...],
                      gsub_ref[...], wo_ref[...], gpost_ref[...])


def _mix_out(x, oa, ob, z, gsub, wo, gpost, *, tm):
    t, d = x.shape
    row = lambda i: (i, 0)
    const = lambda i: (0, 0)
    return pl.pallas_call(
        _mix_out_kernel,
        grid=(t // tm,),
        in_specs=[
            pl.BlockSpec((tm, d), row),
            pl.BlockSpec((tm, d), row),
            pl.BlockSpec((tm, d), row),
            pl.BlockSpec((tm, d), lambda i: (i, 2)),
            pl.BlockSpec((tm, d), lambda i: (i, 3)),
            pl.BlockSpec((tm, d), lambda i: (i, 4)),
            pl.BlockSpec((1, gsub.shape[1]), const),
            pl.BlockSpec((d, d), const),
            pl.BlockSpec((1, d), const),
        ],
        out_specs=pl.BlockSpec((tm, d), row),
        out_shape=jax.ShapeDtypeStruct((t, d), F32),
        compiler_params=_params("parallel"),
        name="mix_out",
    )(x, oa, ob, z, z, z, gsub, wo, gpost)


def _ffn_kernel(x_ref, gpre_ref, wg_ref, wu_ref, wd_ref, gpost_ref, y_ref):
    x = x_ref[...]
    h = _rms(x, gpre_ref[...]).astype(BF16)
    gate = jnp.dot(h, wg_ref[...], preferred_element_type=F32)
    up = jnp.dot(h, wu_ref[...], preferred_element_type=F32)
    act = (jax.nn.silu(gate) * up).astype(BF16)
    f = jnp.dot(act, wd_ref[...], preferred_element_type=F32)
    y_ref[...] = x + _rms(f, gpost_ref[...])


def _ffn(x, gpre, wg, wu, wd, gpost, *, tm):
    t, d = x.shape
    dff = wg.shape[1]
    row = lambda i: (i, 0)
    const = lambda i: (0, 0)
    resident = lambda shape: pl.BlockSpec(shape, const, pipeline_mode=pl.Buffered(1))
    return pl.pallas_call(
        _ffn_kernel,
        grid=(t // tm,),
        in_specs=[
            pl.BlockSpec((tm, d), row),
            pl.BlockSpec((1, d), const),
            resident((d, dff)), resident((d, dff)), resident((dff, d)),
            pl.BlockSpec((1, d), const),
        ],
        out_specs=pl.BlockSpec((tm, d), row),
        out_shape=jax.ShapeDtypeStruct((t, d), F32),
        compiler_params=_params("parallel"),
        name="ffn",
    )(x, gpre, wg, wu, wd, gpost)


def _largest_divisor(n, candidates):
    for c in candidates:
        if n % c == 0:
            return c
    raise ValueError(f"no supported tile for extent {n}")


def _prep_weights(g_mix_pre, w_in, w_gk, b_gk, g_attn_sub, g_gla_sub, w_out, g_mix_post,
                  g_ffn_pre, w_gate, w_up, w_down, g_ffn_post):
    d = w_in.shape[0]
    rank = w_gk.shape[0]
    a_w, wk, wv = d, d // 2, d
    off = [0]
    for s in (a_w, a_w, a_w, wk, wk, wv, rank, wv, d, d):
        off.append(off[-1] + s)
    col = lambda a, b: w_in[:, off[a]:off[b]]
    wq_t = col(0, 1).T.astype(BF16)
    wv_t = col(2, 3).T.astype(BF16)
    wm = jnp.concatenate([col(3, 5), col(5, 6), col(7, 8), col(8, 9), col(9, 10)], axis=1).astype(BF16)
    wg = jnp.pad(col(6, 7), ((0, 0), (0, LANES - rank))).astype(BF16)
    wgk = jnp.pad(w_gk, ((0, LANES - rank), (0, 0))).astype(BF16)
    row = lambda v: v.reshape(1, -1).astype(F32)
    return dict(
        g_mix_pre=row(g_mix_pre), wq_t=wq_t, wv_t=wv_t, wk=col(1, 2).astype(BF16), wv=col(2, 3).astype(BF16),
        wm=wm, wg=wg, wgk=wgk, bgk=row(b_gk),
        g_attn_row=row(g_attn_sub), g_attn_col=g_attn_sub.reshape(-1, 1).astype(F32),
        g_gla_sub=row(g_gla_sub), w_out=w_out.astype(BF16), g_mix_post=row(g_mix_post),
        g_ffn_pre=row(g_ffn_pre), w_gate=w_gate.astype(BF16), w_up=w_up.astype(BF16),
        w_down=w_down.astype(BF16), g_ffn_post=row(g_ffn_post))


def _ffn_block(x1, w):
    tm = _largest_divisor(x1.shape[0], (512, 256, 128, 64, 32, 16, 8))
    return _ffn(x1, w["g_ffn_pre"], w["w_gate"], w["w_up"], w["w_down"], w["g_ffn_post"], tm=tm)


def kernel(x_prompt, x_sample, cache_k, cache_v, state_gla, page_table, g_mix_pre, w_in, w_gk, b_gk,
           lambda_q1, lambda_k1, lambda_q2, lambda_k2, g_attn_sub, g_gla_sub, w_out, g_mix_post,
           g_ffn_pre, w_gate, w_up, w_down, g_ffn_post):
    depth = w_in.shape[0]
    bsz, seq, d = x_prompt.shape
    dbsz, dseq, _ = x_sample.shape
    heads = d // A_DV
    page = cache_k.shape[2]
    n_pool = cache_k.shape[1]
    n_pages = page_table.shape[1]
    dk = d // (2 * B_HEADS)
    dv = d // B_HEADS
    assert bsz == 1, "prompt kernels take one sequence"
    pad_seq = -(-dseq // SUBLANES) * SUBLANES

    x_p = x_prompt.reshape(seq, d)
    x_s = jnp.pad(x_sample, ((0, 0), (0, pad_seq - dseq), (0, 0))).reshape(dbsz * pad_seq, d)

    outs = ([], [], [], [], [], [])
    for l in range(depth):
        lam_init = 0.8 - 0.6 * math.exp(-0.3 * l)
        w = _prep_weights(g_mix_pre[l], w_in[l], w_gk[l], b_gk[l], g_attn_sub[l], g_gla_sub[l], w_out[l],
                          g_mix_post[l], g_ffn_pre[l], w_gate[l], w_up[l], w_down[l], g_ffn_post[l])
        lams = [v[l].reshape(1, A_DQK).astype(F32) for v in (lambda_q1, lambda_k1, lambda_q2, lambda_k2)]
        attn_proj = functools.partial(_attn_proj, g=w["g_mix_pre"], wq_t=w["wq_t"], wv_t=w["wv_t"],
                                      wk=w["wk"], wv=w["wv"])
        mix_proj = functools.partial(_mix_proj, g=w["g_mix_pre"], wg=w["wg"], wgk=w["wgk"], bgk=w["bgk"],
                                     wm=w["wm"])

        qt, vt, kb, k_p, v_p = attn_proj(x_p, tm=_largest_divisor(seq, (512, 256, 128)))
        z, la = mix_proj(x_p, tm=_largest_divisor(seq, (512, 256, 128)))
        blk = _largest_divisor(seq, (512, 256, 128))
        oa = _attn_prompt(qt, kb, vt, *lams, w["g_attn_col"], bq=blk, bk=blk, lam_init=lam_init)
        chunk = _largest_divisor(seq, (128, 64, 32, 16, 8))
        tc = _largest_divisor(seq, (512, 256, 128, 64, 32, 16, 8))
        x_p, s_p = _gla_prompt(x_p, oa, z, la, w["g_gla_sub"], w["w_out"], w["g_mix_post"],
                               tc=tc, chunk=chunk, sub=min(16, chunk))
        x_p = _ffn_block(x_p, w)
        outs[0].append(k_p.reshape(bsz, seq, heads, A_DV))
        outs[1].append(v_p.reshape(bsz, seq, heads, A_DV))
        outs[2].append(s_p.reshape(bsz, B_HEADS, dk, dv))

        ts = dbsz * pad_seq
        tms = _largest_divisor(ts, (256, 128))
        qts, _, _, k_s, v_s = attn_proj(x_s, tm=tms)
        zs, las = mix_proj(x_s, tm=tms)
        zs3 = zs.reshape(dbsz, pad_seq, zs.shape[1])
        q_s = qts.T.reshape(dbsz, pad_seq, d)
        k_s = k_s.reshape(dbsz, pad_seq * heads, A_DV)
        v_s = v_s.reshape(dbsz, pad_seq * heads, A_DV)
        n_group = _largest_divisor(n_pages, (16, 8, 4, 2, 1))
        oas = _attn_decode(page_table, q_s, k_s, v_s, cache_k[l].reshape(n_pool, page * heads, A_DV),
                           cache_v[l].reshape(n_pool, page * heads, A_DV), *lams, w["g_attn_row"],
                           n_group=n_group, lam_init=lam_init)
        obs, s_s = _gla_sample(zs3, las.reshape(dbsz, pad_seq, las.shape[1]), state_gla[l], d=d, n_valid=dseq)
        x_s = _mix_out(x_s, oas.reshape(ts, d), obs.reshape(ts, d), zs, w["g_gla_sub"], w["w_out"],
                       w["g_mix_post"], tm=_largest_divisor(ts, (512, 256, 128, 64, 32, 16, 8)))
        x_s = _ffn_block(x_s, w)
        outs[3].append(k_s.reshape(dbsz, pad_seq, heads, A_DV)[:, :dseq])
        outs[4].append(v_s.reshape(dbsz, pad_seq, heads, A_DV)[:, :dseq])
        outs[5].append(s_s)

    y_p = x_p.reshape(bsz, seq, d)
    y_s = x_s.reshape(dbsz, pad_seq, d)[:, :dseq]
    k_p, v_p, g_p, k_s, v_s, g_s = (jnp.stack(o) for o in outs)
    return (y_p, y_s, k_p, v_p, g_p, k_s, v_s, g_s)
```

```python
import functools
import math

import jax
import jax.numpy as jnp
from jax import lax
from jax.experimental import pallas as pl
from jax.experimental.pallas import tpu as pltpu

A_DQK = 64
A_DV = 2 * A_DQK
B_HEADS = 4
GATE_TAU = 16.0
EPS = 1e-6

LANES = 128
SUBLANES = 8
BF16_SUBLANES = 16
VMEM_LIMIT_BYTES = 56 * 1024 * 1024

NEG_BIG = -1e30
LOG2_E = math.log2(math.e)
ATTN_SCORE_BUFFERS = 8
ATTN_SCORE_LOOKAHEAD = 1
F32 = jnp.float32
BF16 = jnp.bfloat16


def _params(*semantics):
    return pltpu.CompilerParams(dimension_semantics=semantics, vmem_limit_bytes=VMEM_LIMIT_BYTES)


def _rms(x, g):
    return x * lax.rsqrt(jnp.mean(x * x, axis=-1, keepdims=True) + EPS) * g


def _lambda(lq1, lk1, lq2, lk2, lam_init):
    s1 = jnp.sum(lq1 * lk1, axis=-1, keepdims=True)
    s2 = jnp.sum(lq2 * lk2, axis=-1, keepdims=True)
    return jnp.exp(s1) - jnp.exp(s2) + lam_init


def _attn_proj_kernel(x_ref, g_ref, wq_t_ref, wk_ref, wv_ref,
                      qt_ref, vt_ref, kb_ref, ko_ref, vo_ref, *, q_scale, heads):
    tm = x_ref.shape[0]
    h = _rms(x_ref[...], g_ref[...]).astype(BF16)
    nt = (((1,), (1,)), ((), ()))
    qt = lax.dot_general(wq_t_ref[...], h, nt, preferred_element_type=F32)
    qt_ref[...] = (qt * q_scale).astype(BF16)
    k = jnp.dot(h, wk_ref[...], preferred_element_type=F32)
    kb_ref[...] = k.astype(BF16)
    v = jnp.dot(h, wv_ref[...], preferred_element_type=F32)
    vt_ref[...] = v.T.astype(BF16)
    for hd in range(heads):
        ko_ref[pl.ds(hd, tm, stride=heads), :] = k[:, hd * A_DV:(hd + 1) * A_DV]
        vo_ref[pl.ds(hd, tm, stride=heads), :] = v[:, hd * A_DV:(hd + 1) * A_DV]


def _attn_proj(x, g, wq_t, wk, wv, *, tm):
    t, d = x.shape
    heads = d // A_DV
    const = lambda i: (0, 0)
    return pl.pallas_call(
        functools.partial(_attn_proj_kernel, q_scale=A_DQK ** -0.5 * LOG2_E, heads=heads),
        grid=(t // tm,),
        in_specs=[
            pl.BlockSpec((tm, d), lambda i: (i, 0)),
            pl.BlockSpec((1, d), const),
            pl.BlockSpec((d, d), const),
            pl.BlockSpec((d, d), const),
            pl.BlockSpec((d, d), const),
        ],
        out_specs=[
            pl.BlockSpec((d, tm), lambda i: (0, i)),
            pl.BlockSpec((d, tm), lambda i: (0, i)),
            pl.BlockSpec((tm, d), lambda i: (i, 0)),
            pl.BlockSpec((tm * heads, A_DV), lambda i: (i, 0)),
            pl.BlockSpec((tm * heads, A_DV), lambda i: (i, 0)),
        ],
        out_shape=[
            jax.ShapeDtypeStruct((d, t), BF16),
            jax.ShapeDtypeStruct((d, t), BF16),
            jax.ShapeDtypeStruct((t, d), BF16),
            jax.ShapeDtypeStruct((t * heads, A_DV), F32),
            jax.ShapeDtypeStruct((t * heads, A_DV), F32),
        ],
        compiler_params=_params("parallel"),
        name="attn_proj",
    )(x, g, wq_t, wk, wv)


def _mix_proj_kernel(x_ref, g_ref, wg_ref, wgk_ref, bgk_ref, wm_ref, z_ref, la_ref):
    h = _rms(x_ref[...], g_ref[...]).astype(BF16)
    g_lr = jnp.dot(h, wg_ref[...], preferred_element_type=F32)
    gk = jnp.dot(g_lr.astype(BF16), wgk_ref[...], preferred_element_type=F32) + bgk_ref[...]
    la_ref[...] = jax.nn.log_sigmoid(gk) / GATE_TAU
    z_ref[...] = jnp.dot(h, wm_ref[...], preferred_element_type=F32)


def _mix_proj(x, g, wg, wgk, bgk, wm, *, tm):
    t, d = x.shape
    n = wm.shape[1]
    wk = wgk.shape[1]
    const = lambda i: (0, 0)
    return pl.pallas_call(
        _mix_proj_kernel,
        grid=(t // tm,),
        in_specs=[
            pl.BlockSpec((tm, d), lambda i: (i, 0)),
            pl.BlockSpec((1, d), const),
            pl.BlockSpec((d, LANES), const),
            pl.BlockSpec((LANES, wk), const),
            pl.BlockSpec((1, wk), const),
            pl.BlockSpec((d, n), const, pipeline_mode=pl.Buffered(1)),
        ],
        out_specs=[
            pl.BlockSpec((tm, n), lambda i: (i, 0)),
            pl.BlockSpec((tm, wk), lambda i: (i, 0)),
        ],
        out_shape=[
            jax.ShapeDtypeStruct((t, n), F32),
            jax.ShapeDtypeStruct((t, wk), F32),
        ],
        compiler_params=_params("parallel"),
        name="mix_proj",
    )(x, g, wg, wgk, bgk, wm)


def _attn_prompt_kernel(qt_ref, k_ref, vt_ref, lq1_ref, lk1_ref, lq2_ref, lk2_ref, gcol_ref,
                        o_ref, qs_ref, s_refs, c_refs, m_ref, l_ref, acc_ref, *, bq, bk, lam_init):
    n_buf = len(s_refs)
    ratio = bq // bk
    i = pl.program_id(1)
    q = qt_ref[...]
    row = lax.broadcasted_iota(jnp.int32, q.shape, 0)
    zero = jnp.zeros_like(q)
    qs_ref[0] = jnp.where(row < A_DQK, q, zero)
    qs_ref[1] = jnp.where(row >= A_DQK, q, zero)

    m_ref[...] = jnp.full(m_ref.shape, NEG_BIG, F32)
    l_ref[...] = jnp.zeros(l_ref.shape, F32)
    acc_ref[...] = jnp.zeros(acc_ref.shape, F32)

    def scores(kb, slot):
        start = pl.multiple_of(kb * bk, bk)
        k = k_ref[pl.ds(start, bk), :]
        for mi in range(2):
            s = jnp.dot(k, qs_ref[mi], preferred_element_type=F32)
            s_refs[slot][mi] = s
            c_refs[slot][mi] = jnp.max(s, axis=0, keepdims=True)

    def absorb(kb, slot, diag):
        start = pl.multiple_of(kb * bk, bk)
        vt = vt_ref[:, pl.ds(start, bk)]
        vt1 = jnp.concatenate([vt, jnp.ones((BF16_SUBLANES, bk), BF16)], axis=0)
        for mi in range(2):
            s = s_refs[slot][mi]
            if diag is not None:
                kpos = lax.broadcasted_iota(jnp.int32, s.shape, 0) + diag * bk
                qpos = lax.broadcasted_iota(jnp.int32, s.shape, 1)
                s = jnp.where(kpos <= qpos, s, NEG_BIG)
            m_old = m_ref[mi]
            cmax = c_refs[slot][mi] if diag is None else jnp.max(s, axis=0, keepdims=True)
            m_new = jnp.maximum(m_old, cmax)
            alpha = jnp.exp2(m_old - m_new)
            p = jnp.exp2(s - m_new).astype(BF16)
            upd = jnp.dot(vt1, p, preferred_element_type=F32)
            l_ref[mi] = alpha * l_ref[mi] + upd[A_DV:A_DV + 1]
            acc_ref[mi] = alpha * acc_ref[mi] + upd[:A_DV]
            m_ref[mi] = m_new

    ahead = ATTN_SCORE_LOOKAHEAD
    assert ahead <= ratio and ahead < n_buf
    for j in range(ahead):
        scores(j, j)

    n_plain = i * ratio
    n_full = n_plain // n_buf

    def full_round(r, carry):
        kb = r * n_buf
        for j in range(n_buf):
            scores(kb + j + ahead, (j + ahead) % n_buf)
            absorb(kb + j, j, None)
        return carry

    lax.fori_loop(0, n_full, full_round, 0)

    kb0 = n_full * n_buf
    rest = n_plain + ratio - kb0
    for count in range(ratio, ratio + n_buf):
        def tail(count=count):
            for j in range(count):
                if j + ahead < count:
                    scores(kb0 + j + ahead, (j + ahead) % n_buf)
                absorb(kb0 + j, j % n_buf, None if j < count - ratio else j - (count - ratio))
        pl.when(rest == count)(tail)

    lam = _lambda(lq1_ref[...], lk1_ref[...], lq2_ref[...], lk2_ref[...], lam_init)
    o = acc_ref[0] * (1.0 / l_ref[0]) - acc_ref[1] * (lam / l_ref[1])
    ms = jnp.mean(o * o, axis=0, keepdims=True)
    y = o * lax.rsqrt(ms + EPS) * gcol_ref[...] * (1.0 - lam_init)
    o_ref[...] = y.T


def _attn_prompt(qt, kb, vt, lq1, lk1, lq2, lk2, gcol, *, bq, bk, lam_init):
    d, t = qt.shape
    heads = d // A_DV
    vec = pl.BlockSpec((1, A_DQK), lambda h, i: (0, 0))
    return pl.pallas_call(
        functools.partial(_attn_prompt_kernel, bq=bq, bk=bk, lam_init=lam_init),
        grid=(heads, t // bq),
        in_specs=[
            pl.BlockSpec((A_DV, bq), lambda h, i: (h, i)),
            pl.BlockSpec((t, A_DV), lambda h, i: (0, h)),
            pl.BlockSpec((A_DV, t), lambda h, i: (h, 0)),
            vec, vec, vec, vec,
            pl.BlockSpec((A_DV, 1), lambda h, i: (0, 0)),
        ],
        out_specs=pl.BlockSpec((bq, A_DV), lambda h, i: (i, h)),
        out_shape=jax.ShapeDtypeStruct((t, d), F32),
        scratch_shapes=[
            pltpu.VMEM((2, A_DV, bq), BF16),
            tuple(pltpu.VMEM((2, bk, bq), F32) for _ in range(ATTN_SCORE_BUFFERS)),
            tuple(pltpu.VMEM((2, 1, bq), F32) for _ in range(ATTN_SCORE_BUFFERS)),
            pltpu.VMEM((2, 1, bq), F32),
            pltpu.VMEM((2, 1, bq), F32),
            pltpu.VMEM((2, A_DV, bq), F32),
        ],
        compiler_params=_params("parallel", "arbitrary"),
        name="attn_prompt",
    )(qt, kb, vt, lq1, lk1, lq2, lk2, gcol)


def _attn_decode_kernel(pt_ref, q_ref, kn_ref, vn_ref, lq1_ref, lk1_ref, lq2_ref, lk2_ref, g_ref,
                        *rest, n_group, n_tok, lam_init):
    k_refs = rest[:n_group]
    v_refs = rest[n_group:2 * n_group]
    o_ref, qbd_ref, m_ref, l_ref, acc_ref, kpage_ref, vpage_ref, kcat_ref, vcat_ref = rest[2 * n_group:]
    j = pl.program_id(1)
    rows, d = qbd_ref.shape
    page = kpage_ref.shape[0]
    heads = d // A_DV

    def update(k, v, mask):
        nt = (((1,), (1,)), ((), ()))
        s = lax.dot_general(qbd_ref[...], k, nt, preferred_element_type=F32)
        if mask is not None:
            s = jnp.where(mask, s, NEG_BIG)
        m_old = m_ref[...]
        m_new = jnp.maximum(m_old, jnp.max(s, axis=1, keepdims=True))
        alpha = jnp.exp2(m_old - m_new)
        p = jnp.exp2(s - m_new)
        l_ref[...] = alpha * l_ref[...] + jnp.sum(p, axis=1, keepdims=True)
        acc_ref[...] = alpha * acc_ref[...] + jnp.dot(p.astype(BF16), v, preferred_element_type=F32)
        m_ref[...] = m_new

    def gather_heads(ref, n_keys):
        cols = [ref[0, pl.ds(h, n_keys, stride=heads), :] for h in range(heads)]
        return jnp.concatenate(cols, axis=1).astype(BF16)

    @pl.when(j == 0)
    def _():
        q = q_ref[0]
        qrep = jnp.concatenate([q] * (rows // n_tok), axis=0)
        r = lax.broadcasted_iota(jnp.int32, (rows, d), 0)
        c = lax.broadcasted_iota(jnp.int32, (rows, d), 1)
        qbd_ref[...] = jnp.where(r // n_tok == c // A_DQK, qrep, jnp.zeros_like(qrep))
        m_ref[...] = jnp.full(m_ref.shape, NEG_BIG, F32)
        l_ref[...] = jnp.zeros(l_ref.shape, F32)
        acc_ref[...] = jnp.zeros(acc_ref.shape, F32)
        kpage_ref[...] = jnp.zeros(kpage_ref.shape, BF16)
        vpage_ref[...] = jnp.zeros(vpage_ref.shape, BF16)
        kpage_ref[0:n_tok, :] = gather_heads(kn_ref, n_tok)
        vpage_ref[0:n_tok, :] = gather_heads(vn_ref, n_tok)
        tq = lax.broadcasted_iota(jnp.int32, (rows, page), 0) % n_tok
        ts = lax.broadcasted_iota(jnp.int32, (rows, page), 1)
        update(kpage_ref[...], vpage_ref[...], ts <= tq)

    def load_pages(refs, dst_ref):
        for g_idx, ref in enumerate(refs):
            dst_ref[g_idx * page:(g_idx + 1) * page, :] = gather_heads(ref, page)
        return dst_ref[...]

    update(load_pages(k_refs, kcat_ref), load_pages(v_refs, vcat_ref), None)

    @pl.when(j == pl.num_programs(1) - 1)
    def _():
        lam = _lambda(lq1_ref[...], lk1_ref[...], lq2_ref[...], lk2_ref[...], lam_init)
        on = acc_ref[...] / l_ref[...]
        g = g_ref[...]
        for h in range(heads):
            r0 = h * 2 * n_tok
            c0 = h * A_DV
            o = on[r0:r0 + n_tok, c0:c0 + A_DV] - lam * on[r0 + n_tok:r0 + 2 * n_tok, c0:c0 + A_DV]
            o_ref[0, :, c0:c0 + A_DV] = _rms(o, g) * (1.0 - lam_init)


def _attn_decode(page_table, q, k_new, v_new, cache_k, cache_v, lq1, lk1, lq2, lk2, g, *, n_group, lam_init):
    bsz, n_tok, d = q.shape
    n_pages = page_table.shape[1]
    heads = d // A_DV
    page = cache_k.shape[1] // heads
    rows = (d // A_DQK) * n_tok
    vec = pl.BlockSpec((1, A_DQK), lambda b, j, pt: (0, 0))

    def page_spec(g_idx):
        return pl.BlockSpec((1, page * heads, A_DV), lambda b, j, pt: (pt[b, j * n_group + g_idx], 0, 0))

    grid_spec = pltpu.PrefetchScalarGridSpec(
        num_scalar_prefetch=1,
        grid=(bsz, n_pages // n_group),
        in_specs=[
            pl.BlockSpec((1, n_tok, d), lambda b, j, pt: (b, 0, 0)),
            pl.BlockSpec((1, n_tok * heads, A_DV), lambda b, j, pt: (b, 0, 0)),
            pl.BlockSpec((1, n_tok * heads, A_DV), lambda b, j, pt: (b, 0, 0)),
            vec, vec, vec, vec,
            pl.BlockSpec((1, A_DV), lambda b, j, pt: (0, 0)),
        ] + [page_spec(g_idx) for g_idx in range(n_group)] * 2,
        out_specs=pl.BlockSpec((1, n_tok, d), lambda b, j, pt: (b, 0, 0)),
        scratch_shapes=[
            pltpu.VMEM((rows, d), BF16),
            pltpu.VMEM((rows, 1), F32),
            pltpu.VMEM((rows, 1), F32),
            pltpu.VMEM((rows, d), F32),
            pltpu.VMEM((page, d), BF16),
            pltpu.VMEM((page, d), BF16),
            pltpu.VMEM((n_group * page, d), BF16),
            pltpu.VMEM((n_group * page, d), BF16),
        ],
    )
    return pl.pallas_call(
        functools.partial(_attn_decode_kernel, n_group=n_group, n_tok=n_tok, lam_init=lam_init),
        grid_spec=grid_spec,
        out_shape=jax.ShapeDtypeStruct((bsz, n_tok, d), F32),
        compiler_params=_params("parallel", "arbitrary"),
        name="attn_decode",
    )(page_table, q, k_new, v_new, lq1, lk1, lq2, lk2, g, *([cache_k] * n_group), *([cache_v] * n_group))


def _cumsum_rows(la):
    c = la.shape[0]
    t_i = lax.broadcasted_iota(jnp.int32, (c, c), 0)
    s_i = lax.broadcasted_iota(jnp.int32, (c, c), 1)
    tri = (s_i <= t_i).astype(F32)
    return jnp.dot(tri, la, precision=lax.Precision.HIGHEST, preferred_element_type=F32)


def _gla_chunk(q, k, v, b, st, *, sub, scale):
    c, dk = q.shape
    q = q * scale
    b = b * LOG2_E
    nt = (((1,), (1,)), ((), ()))

    o = lax.dot_general((q * jnp.exp2(b)).astype(BF16), st.astype(BF16), nt, preferred_element_type=F32)

    lane = lax.broadcasted_iota(jnp.int32, (sub, c), 1)
    trow = lax.broadcasted_iota(jnp.int32, (sub, c), 0)
    a_rows = []
    for blk in range(c // sub):
        r0 = blk * sub
        qi = q[r0:r0 + sub]
        bi = b[r0:r0 + sub]
        a_blk = jnp.zeros((sub, c), F32)
        if blk > 0:
            ref = b[r0:r0 + 1]
            qs = qi * jnp.exp2(bi - ref)
            ks = k * jnp.exp2(jnp.minimum(ref - b, 0.0))
            a_off = lax.dot_general(qs.astype(BF16), ks.astype(BF16), nt, preferred_element_type=F32)
            a_blk = jnp.where(lane < r0, a_off, a_blk)
        for s in range(sub):
            ks_row = k[r0 + s:r0 + s + 1]
            bs_row = b[r0 + s:r0 + s + 1]
            x = qi * ks_row * jnp.exp2(bi - bs_row)
            col = jnp.sum(x, axis=1, keepdims=True)
            a_blk = jnp.where((lane == r0 + s) & (trow >= s), col, a_blk)
        a_rows.append(a_blk)
    a = a_rows[0] if len(a_rows) == 1 else jnp.concatenate(a_rows, axis=0)
    o = o + jnp.dot(a.astype(BF16), v.astype(BF16), preferred_element_type=F32)

    bl = b[c - 1:c]
    kd = k * jnp.exp2(bl - b)
    tn = (((0,), (0,)), ((), ()))
    st_new = st * jnp.exp2(bl) + lax.dot_general(v.astype(BF16), kd.astype(BF16), tn, preferred_element_type=F32)
    return o, st_new


def _gla_prompt_kernel(qk_ref, v_ref, la_ref, x_ref, oa_ref, r_ref, ga_ref, gb_ref, gsub_ref, wo_ref, gpost_ref,
                       y_ref, s_ref, st_ref, o_ref, *, chunk, sub, scale):
    t = pl.program_id(0)
    heads, dv, dk = st_ref.shape

    @pl.when(t == 0)
    def _():
        st_ref[...] = jnp.zeros(st_ref.shape, F32)

    def body(ci, carry):
        r0 = pl.multiple_of(ci * chunk, chunk)
        rows = pl.ds(r0, chunk)
        b_all = _cumsum_rows(la_ref[rows, :])
        for h in range(heads):
            o, st_new = _gla_chunk(qk_ref[rows, h * dk:(h + 1) * dk],
                                   qk_ref[rows, (heads + h) * dk:(heads + h + 1) * dk],
                                   v_ref[rows, h * dv:(h + 1) * dv],
                                   b_all[:, h * dk:(h + 1) * dk], st_ref[h], sub=sub, scale=scale)
            o_ref[rows, h * dv:(h + 1) * dv] = o
            st_ref[h] = st_new
        return carry

    lax.fori_loop(0, qk_ref.shape[0] // chunk, body, 0)

    @pl.when(t == pl.num_programs(0) - 1)
    def _():
        for h in range(heads):
            s_ref[h] = st_ref[h].T

    y_ref[...] = _mix(x_ref[...], oa_ref[...], o_ref[...], r_ref[...], ga_ref[...], gb_ref[...],
                      gsub_ref[...], wo_ref[...], gpost_ref[...])


def _gla_prompt(x, oa, z, la, gsub, wo, gpost, *, tc, chunk, sub):
    t, d = x.shape
    dk = d // (2 * B_HEADS)
    dv = d // B_HEADS
    const = lambda i: (0, 0)
    tile = lambda j: pl.BlockSpec((tc, d), lambda i: (i, j))
    return pl.pallas_call(
        functools.partial(_gla_prompt_kernel, chunk=chunk, sub=sub, scale=dk ** -0.5),
        grid=(t // tc,),
        in_specs=[
            tile(0), tile(1),
            pl.BlockSpec((tc, B_HEADS * dk), lambda i: (i, 0)),
            tile(0), tile(0), tile(2), tile(3), tile(4),
            pl.BlockSpec((1, gsub.shape[1]), const),
            pl.BlockSpec((d, d), const),
            pl.BlockSpec((1, d), const),
        ],
        out_specs=[
            pl.BlockSpec((tc, d), lambda i: (i, 0)),
            pl.BlockSpec((B_HEADS, dk, dv), lambda i: (0, 0, 0)),
        ],
        out_shape=[
            jax.ShapeDtypeStruct((t, d), F32),
            jax.ShapeDtypeStruct((B_HEADS, dk, dv), F32),
        ],
        scratch_shapes=[pltpu.VMEM((B_HEADS, dv, dk), F32), pltpu.VMEM((tc, d), F32)],
        compiler_params=_params("arbitrary"),
        name="gla_prompt",
    )(z, z, la, x, oa, z, z, z, gsub, wo, gpost)


def _gla_sample_kernel(qk_ref, v_ref, la_ref, s0_ref, o_ref, s_ref, *, n_valid, scale):
    c = qk_ref.shape[1]
    heads, dk, dv = s0_ref.shape[1:]
    valid = lax.broadcasted_iota(jnp.int32, (c, 1), 0) < n_valid
    b_all = _cumsum_rows(jnp.where(valid, la_ref[0], 0.0))
    for h in range(heads):
        k = jnp.where(valid, qk_ref[0, :, (heads + h) * dk:(heads + h + 1) * dk], 0.0)
        o, st_new = _gla_chunk(qk_ref[0, :, h * dk:(h + 1) * dk], k, v_ref[0, :, h * dv:(h + 1) * dv],
                               b_all[:, h * dk:(h + 1) * dk], s0_ref[0, h].T, sub=c, scale=scale)
        o_ref[0, :, h * dv:(h + 1) * dv] = o
        s_ref[0, h] = st_new.T


def _gla_sample(z3, la3, s0, *, d, n_valid):
    bsz, c, _ = z3.shape
    state_spec = pl.BlockSpec((1,) + s0.shape[1:], lambda b: (b, 0, 0, 0))
    return pl.pallas_call(
        functools.partial(_gla_sample_kernel, n_valid=n_valid, scale=s0.shape[2] ** -0.5),
        grid=(bsz,),
        in_specs=[
            pl.BlockSpec((1, c, d), lambda b: (b, 0, 0)),
            pl.BlockSpec((1, c, d), lambda b: (b, 0, 1)),
            pl.BlockSpec((1, c, la3.shape[2]), lambda b: (b, 0, 0)),
            state_spec,
        ],
        out_specs=[pl.BlockSpec((1, c, d), lambda b: (b, 0, 0)), state_spec],
        out_shape=[
            jax.ShapeDtypeStruct((bsz, c, d), F32),
            jax.ShapeDtypeStruct(s0.shape, F32),
        ],
        compiler_params=_params("parallel"),
        name="gla_sample",
    )(z3, z3, la3, s0)


def _mix(x, oa, ob, r, ga, gb, gsub, wo, gpost):
    dv = gsub.shape[1]
    ob_n = jnp.concatenate(
        [_rms(ob[:, h * dv:(h + 1) * dv], gsub) for h in range(ob.shape[1] // dv)], axis=1)
    ob_n = ob_n * jax.nn.silu(r)
    merged = jax.nn.sigmoid(ga) * oa + jax.nn.sigmoid(gb) * ob_n
    y = jnp.dot(merged.astype(BF16), wo, preferred_element_type=F32)
    return x + _rms(y, gpost)


def _mix_out_kernel(x_ref, oa_ref, ob_ref, r_ref, ga_ref, gb_ref, gsub_ref, wo_ref, gpost_ref, y_ref):
    y_ref[...] = _mix(x_ref[...], oa_ref[...], ob_ref[...], r_ref[...], ga_ref[...], gb_ref[...],
                      gsub_ref[...], wo_ref[...], gpost_ref[...])


def _mix_out(x, oa, ob, z, gsub, wo, gpost, *, tm):
    t, d = x.shape
    row = lambda i: (i, 0)
    const = lambda i: (0, 0)
    return pl.pallas_call(
        _mix_out_kernel,
        grid=(t // tm,),
        in_specs=[
            pl.BlockSpec((tm, d), row),
            pl.BlockSpec((tm, d), row),
            pl.BlockSpec((tm, d), row),
            pl.BlockSpec((tm, d), lambda i: (i, 2)),
            pl.BlockSpec((tm, d), lambda i: (i, 3)),
            pl.BlockSpec((tm, d), lambda i: (i, 4)),
            pl.BlockSpec((1, gsub.shape[1]), const),
            pl.BlockSpec((d, d), const),
            pl.BlockSpec((1, d), const),
        ],
        out_specs=pl.BlockSpec((tm, d), row),
        out_shape=jax.ShapeDtypeStruct((t, d), F32),
        compiler_params=_params("parallel"),
        name="mix_out",
    )(x, oa, ob, z, z, z, gsub, wo, gpost)


def _ffn_kernel(x_ref, gpre_ref, wg_ref, wu_ref, wd_ref, gpost_ref, y_ref):
    x = x_ref[...]
    h = _rms(x, gpre_ref[...]).astype(BF16)
    gate = jnp.dot(h, wg_ref[...], preferred_element_type=F32)
    up = jnp.dot(h, wu_ref[...], preferred_element_type=F32)
    act = (jax.nn.silu(gate) * up).astype(BF16)
    f = jnp.dot(act, wd_ref[...], preferred_element_type=F32)
    y_ref[...] = x + _rms(f, gpost_ref[...])


def _ffn(x, gpre, wg, wu, wd, gpost, *, tm):
    t, d = x.shape
    dff = wg.shape[1]
    row = lambda i: (i, 0)
    const = lambda i: (0, 0)
    resident = lambda shape: pl.BlockSpec(shape, const, pipeline_mode=pl.Buffered(1))
    return pl.pallas_call(
        _ffn_kernel,
        grid=(t // tm,),
        in_specs=[
            pl.BlockSpec((tm, d), row),
            pl.BlockSpec((1, d), const),
            resident((d, dff)), resident((d, dff)), resident((dff, d)),
            pl.BlockSpec((1, d), const),
        ],
        out_specs=pl.BlockSpec((tm, d), row),
        out_shape=jax.ShapeDtypeStruct((t, d), F32),
        compiler_params=_params("parallel"),
        name="ffn",
    )(x, gpre, wg, wu, wd, gpost)


def _largest_divisor(n, candidates):
    for c in candidates:
        if n % c == 0:
            return c
    raise ValueError(f"no supported tile for extent {n}")


def _prep_weights(g_mix_pre, w_in, w_gk, b_gk, g_attn_sub, g_gla_sub, w_out, g_mix_post,
                  g_ffn_pre, w_gate, w_up, w_down, g_ffn_post):
    d = w_in.shape[0]
    rank = w_gk.shape[0]
    a_w, wk, wv = d, d // 2, d
    off = [0]
    for s in (a_w, a_w, a_w, wk, wk, wv, rank, wv, d, d):
        off.append(off[-1] + s)
    col = lambda a, b: w_in[:, off[a]:off[b]]
    wq_t = col(0, 1).T.astype(BF16)
    wm = jnp.concatenate([col(3, 5), col(5, 6), col(7, 8), col(8, 9), col(9, 10)], axis=1).astype(BF16)
    wg = jnp.pad(col(6, 7), ((0, 0), (0, LANES - rank))).astype(BF16)
    wgk = jnp.pad(w_gk, ((0, LANES - rank), (0, 0))).astype(BF16)
    row = lambda v: v.reshape(1, -1).astype(F32)
    return dict(
        g_mix_pre=row(g_mix_pre), wq_t=wq_t, wk=col(1, 2).astype(BF16), wv=col(2, 3).astype(BF16),
        wm=wm, wg=wg, wgk=wgk, bgk=row(b_gk),
        g_attn_row=row(g_attn_sub), g_attn_col=g_attn_sub.reshape(-1, 1).astype(F32),
        g_gla_sub=row(g_gla_sub), w_out=w_out.astype(BF16), g_mix_post=row(g_mix_post),
        g_ffn_pre=row(g_ffn_pre), w_gate=w_gate.astype(BF16), w_up=w_up.astype(BF16),
        w_down=w_down.astype(BF16), g_ffn_post=row(g_ffn_post))


def _ffn_block(x1, w):
    tm = _largest_divisor(x1.shape[0], (512, 256, 128, 64, 32, 16, 8))
    return _ffn(x1, w["g_ffn_pre"], w["w_gate"], w["w_up"], w["w_down"], w["g_ffn_post"], tm=tm)


def kernel(x_prompt, x_sample, cache_k, cache_v, state_gla, page_table, g_mix_pre, w_in, w_gk, b_gk,
           lambda_q1, lambda_k1, lambda_q2, lambda_k2, g_attn_sub, g_gla_sub, w_out, g_mix_post,
           g_ffn_pre, w_gate, w_up, w_down, g_ffn_post):
    depth = w_in.shape[0]
    bsz, seq, d = x_prompt.shape
    dbsz, dseq, _ = x_sample.shape
    heads = d // A_DV
    page = cache_k.shape[2]
    n_pool = cache_k.shape[1]
    n_pages = page_table.shape[1]
    dk = d // (2 * B_HEADS)
    dv = d // B_HEADS
    assert bsz == 1, "prompt kernels take one sequence"
    pad_seq = -(-dseq // SUBLANES) * SUBLANES

    x_p = x_prompt.reshape(seq, d)
    x_s = jnp.pad(x_sample, ((0, 0), (0, pad_seq - dseq), (0, 0))).reshape(dbsz * pad_seq, d)

    outs = ([], [], [], [], [], [])
    for l in range(depth):
        lam_init = 0.8 - 0.6 * math.exp(-0.3 * l)
        w = _prep_weights(g_mix_pre[l], w_in[l], w_gk[l], b_gk[l], g_attn_sub[l], g_gla_sub[l], w_out[l],
                          g_mix_post[l], g_ffn_pre[l], w_gate[l], w_up[l], w_down[l], g_ffn_post[l])
        lams = [v[l].reshape(1, A_DQK).astype(F32) for v in (lambda_q1, lambda_k1, lambda_q2, lambda_k2)]
        attn_proj = functools.partial(_attn_proj, g=w["g_mix_pre"], wq_t=w["wq_t"], wk=w["wk"], wv=w["wv"])
        mix_proj = functools.partial(_mix_proj, g=w["g_mix_pre"], wg=w["wg"], wgk=w["wgk"], bgk=w["bgk"],
                                     wm=w["wm"])

        qt, vt, kb, k_p, v_p = attn_proj(x_p, tm=_largest_divisor(seq, (512, 256, 128)))
        z, la = mix_proj(x_p, tm=_largest_divisor(seq, (512, 256, 128)))
        blk = _largest_divisor(seq, (512, 256, 128))
        oa = _attn_prompt(qt, kb, vt, *lams, w["g_attn_col"], bq=blk, bk=blk, lam_init=lam_init)
        chunk = _largest_divisor(seq, (128, 64, 32, 16, 8))
        tc = _largest_divisor(seq, (512, 256, 128, 64, 32, 16, 8))
        x_p, s_p = _gla_prompt(x_p, oa, z, la, w["g_gla_sub"], w["w_out"], w["g_mix_post"],
                               tc=tc, chunk=chunk, sub=min(16, chunk))
        x_p = _ffn_block(x_p, w)
        outs[0].append(k_p.reshape(bsz, seq, heads, A_DV))
        outs[1].append(v_p.reshape(bsz, seq, heads, A_DV))
        outs[2].append(s_p.reshape(bsz, B_HEADS, dk, dv))

        ts = dbsz * pad_seq
        tms = _largest_divisor(ts, (256, 128))
        qts, _, _, k_s, v_s = attn_proj(x_s, tm=tms)
        zs, las = mix_proj(x_s, tm=tms)
        zs3 = zs.reshape(dbsz, pad_seq, zs.shape[1])
        q_s = qts.T.reshape(dbsz, pad_seq, d)
        k_s = k_s.reshape(dbsz, pad_seq * heads, A_DV)
        v_s = v_s.reshape(dbsz, pad_seq * heads, A_DV)
        n_group = _largest_divisor(n_pages, (16, 8, 4, 2, 1))
        oas = _attn_decode(page_table, q_s, k_s, v_s, cache_k[l].reshape(n_pool, page * heads, A_DV),
                           cache_v[l].reshape(n_pool, page * heads, A_DV), *lams, w["g_attn_row"],
                           n_group=n_group, lam_init=lam_init)
        obs, s_s = _gla_sample(zs3, las.reshape(dbsz, pad_seq, las.shape[1]), state_gla[l], d=d, n_valid=dseq)
        x_s = _mix_out(x_s, oas.reshape(ts, d), obs.reshape(ts, d), zs, w["g_gla_sub"], w["w_out"],
                       w["g_mix_post"], tm=_largest_divisor(ts, (512, 256, 128, 64, 32, 16, 8)))
        x_s = _ffn_block(x_s, w)
        outs[3].append(k_s.reshape(dbsz, pad_seq, heads, A_DV)[:, :dseq])
        outs[4].append(v_s.reshape(dbsz, pad_seq, heads, A_DV)[:, :dseq])
        outs[5].append(s_s)

    y_p = x_p.reshape(bsz, seq, d)
    y_s = x_s.reshape(dbsz, pad_seq, d)[:, :dseq]
    k_p, v_p, g_p, k_s, v_s, g_s = (jnp.stack(o) for o in outs)
    return (y_p, y_s, k_p, v_p, g_p, k_s, v_s, g_s)
```
